```python
import math
import jax
import jax.numpy as jnp
from jax import lax
import numpy as np

D_MODEL = 4096
BATCH = 8
SEQ = 2048
DEPTH = 4
DEC_BATCH = 1
DEC_SEQ = 16384
PAST_LEN = 128

N_MIXERS = 3
RMS_EPS = 1e-6
ROPE_THETA = 500000.0
Q_BLOCK = 128
FFN_HIDDEN = -(-(8 * D_MODEL) // (3 * 256)) * 256

RWKV_HEAD_DIM = 64
RWKV_HEADS = D_MODEL // RWKV_HEAD_DIM
RWKV_DECAY_LORA = max(32, int(round(1.8 * D_MODEL ** 0.5 / 32)) * 32)
RWKV_A_LORA = max(32, int(round(1.8 * D_MODEL ** 0.5 / 32)) * 32)
RWKV_V_LORA = max(32, int(round(1.3 * D_MODEL ** 0.5 / 32)) * 32)
RWKV_G_LORA = max(32, int(round(0.6 * D_MODEL ** 0.8 / 32)) * 32)
RWKV_GN_EPS = 64e-5

MLA_HEADS = 64
MLA_Q_RANK = 1536
MLA_KV_RANK = 512
MLA_NOPE_DIM = 128
MLA_ROPE_DIM = 64
MLA_V_DIM = 128

DIFF_HEAD_DIM = 128
DIFF_HEADS = D_MODEL // (2 * DIFF_HEAD_DIM)
DIFF_ROT_DIM = DIFF_HEAD_DIM // 4
DIFF_EPS = 1e-5

kernel_name = 'hybrid_rwkv7_mla_diffattn_encoder'


def _rmsnorm(x, g, eps=RMS_EPS):
    xf = x.astype(jnp.float32)
    y = xf * lax.rsqrt(jnp.mean(xf * xf, axis=-1, keepdims=True) + eps)
    return (y * g.astype(jnp.float32)).astype(x.dtype)


def _rope_tables(T, dim):
    inv_freq = 1.0 / (ROPE_THETA ** (jnp.arange(0, dim, 2, dtype=jnp.float32) / dim))
    ang = jnp.arange(T, dtype=jnp.float32)[:, None] * inv_freq[None, :]
    return jnp.cos(ang), jnp.sin(ang)


def _apply_rope(x, cos, sin):
    half = x.shape[-1] // 2
    bshape = (1, x.shape[1]) + (1,) * (x.ndim - 3) + (half,)
    c, s = cos.reshape(bshape), sin.reshape(bshape)
    x1 = x[..., :half].astype(jnp.float32)
    x2 = x[..., half:].astype(jnp.float32)
    return jnp.concatenate([x1 * c - x2 * s, x2 * c + x1 * s], axis=-1).astype(x.dtype)


def _partial_rope(x, cos, sin):
    return jnp.concatenate([_apply_rope(x[..., :DIFF_ROT_DIM], cos, sin), x[..., DIFF_ROT_DIM:]], axis=-1)


def _query_blocks(t):
    B, T = t.shape[0], t.shape[1]
    return jnp.moveaxis(t.reshape((B, T // Q_BLOCK, Q_BLOCK) + t.shape[2:]), 1, 0)


def _merge_blocks(o):
    nb, B, qb = o.shape[0], o.shape[1], o.shape[2]
    return jnp.moveaxis(o, 0, 1).reshape((B, nb * qb) + o.shape[3:])


def _swiglu(h, w_gu, w_down):
    gu = h @ w_gu
    return (jax.nn.silu(gu[..., :FFN_HIDDEN]) * gu[..., FFN_HIDDEN:]) @ w_down


def _wkv7_scan(r, w, k, v, a, b, reverse):
    B, T, H, N = r.shape

    def step(S, inp):
        r_t, w_t, k_t, v_t, a_t, b_t = inp
        sa = jnp.einsum('bhvk,bhk->bhv', S, a_t)
        S = S * w_t[:, :, None, :] + sa[..., None] * b_t[:, :, None, :] + v_t[..., None] * k_t[:, :, None, :]
        return S, jnp.einsum('bhvk,bhk->bhv', S, r_t)

    xs = tuple(jnp.swapaxes(t, 0, 1) for t in (r, w, k, v, a, b))
    _, o = lax.scan(step, jnp.zeros((B, H, N, N), jnp.float32), xs, reverse=reverse)
    return jnp.swapaxes(o, 0, 1)


def _rwkv7_mixer(h, p, v_first):
    B, T, C = h.shape
    H, N = RWKV_HEADS, RWKV_HEAD_DIM
    f32 = jnp.float32
    hp = jnp.pad(h, ((0, 0), (1, 1), (0, 0)))
    dx = 0.5 * (hp[:, :-2] + hp[:, 2:]) - h
    mix = p['mix']
    xr, xw, xk, xv, xa, xg = (h + dx * mix[i] for i in range(6))
    rkv = jnp.einsum('gbtc,gcd->gbtd', jnp.stack([xr, xk, xv]), p['w_rkv'])
    r, k, v = rkv[0], rkv[1], rkv[2]
    wl = jnp.einsum('dbtl,dlc->dbtc', jnp.tanh(jnp.einsum('btc,dcl->dbtl', xw, p['w_decay1'])), p['w_decay2'])
    w_log = -jax.nn.softplus(-(p['w_decay0'][:, None, None, :] + wl).astype(f32)) - 0.5
    decay = jnp.exp(-jnp.exp(w_log))
    al = jnp.einsum('dbtl,dlc->dbtc', jnp.einsum('btc,dcl->dbtl', xa, p['a1']), p['a2'])
    a = jax.nn.sigmoid((p['a0'][:, None, None, :] + al).astype(f32))
    if v_first is not None:
        v = v + (v_first - v) * jax.nn.sigmoid(p['v0'] + (xv @ p['v1']) @ p['v2'])
    g = jax.nn.sigmoid(xg @ p['g1']) @ p['g2']
    kk = (k * p['k_k']).reshape(B, T, H, N).astype(f32)
    kk = kk / jnp.maximum(jnp.sqrt(jnp.sum(kk * kk, axis=-1, keepdims=True)), 1e-12)
    k_dir = k.astype(f32)[None] * (1.0 + (a - 1.0) * p['k_a'].astype(f32))

    def heads(t):
        return t.reshape(t.shape[:-1] + (H, N))

    r_h, v_h = heads(r.astype(f32)), heads(v.astype(f32))
    o = (_wkv7_scan(r_h, heads(decay[0]), heads(k_dir[0]), v_h, -kk, kk * heads(a[0]), reverse=False)
         + _wkv7_scan(r_h, heads(decay[1]), heads(k_dir[1]), v_h, -kk, kk * heads(a[1]), reverse=True))
    mu = jnp.mean(o, axis=-1, keepdims=True)
    var = jnp.mean(jnp.square(o - mu), axis=-1, keepdims=True)
    o = ((o - mu) * lax.rsqrt(var + RWKV_GN_EPS)).reshape(B, T, C) * p['lnx_w'] + p['lnx_b']
    bonus = jnp.sum(r_h * heads(k_dir[0] + k_dir[1]) * p['r_k'], axis=-1, keepdims=True) * v_h
    out = ((o + bonus.reshape(B, T, C)) * g).astype(h.dtype) @ p['w_o']
    return out, v


def _mla_mixer(h, p, cos, sin):
    B, T, C = h.shape
    H = MLA_HEADS
    f32 = jnp.float32
    lat = h @ p['w_in']
    c_q = _rmsnorm(lat[..., :MLA_Q_RANK], p['q_norm'])
    c_kv = _rmsnorm(lat[..., MLA_Q_RANK:MLA_Q_RANK + MLA_KV_RANK], p['kv_norm'])
    k_rope = _apply_rope(lat[..., MLA_Q_RANK + MLA_KV_RANK:], cos, sin)
    q = (c_q @ p['w_uq']).reshape(B, T, H, MLA_NOPE_DIM + MLA_ROPE_DIM)
    q_nope = q[..., :MLA_NOPE_DIM]
    q_rope = _apply_rope(q[..., MLA_NOPE_DIM:], cos, sin)
    kv = (c_kv @ p['w_ukv']).reshape(B, T, H, MLA_NOPE_DIM + MLA_V_DIM)
    k_nope, v = kv[..., :MLA_NOPE_DIM], kv[..., MLA_NOPE_DIM:]
    scale = (MLA_NOPE_DIM + MLA_ROPE_DIM) ** -0.5

    def attend(blk):
        qn, qr = blk
        s = (jnp.einsum('bqhd,bkhd->bhqk', qn, k_nope, preferred_element_type=f32)
             + jnp.einsum('bqhr,bkr->bhqk', qr, k_rope, preferred_element_type=f32)) * scale
        pr = jax.nn.softmax(s, axis=-1).astype(v.dtype)
        return jnp.einsum('bhqk,bkhd->bqhd', pr, v)

    o = _merge_blocks(lax.map(attend, (_query_blocks(q_nope), _query_blocks(q_rope))))
    return o.reshape(B, T, H * MLA_V_DIM) @ p['w_o']


def _diff_mixer(h, p, cos, sin, lambda_init):
    B, T, C = h.shape
    H, d = DIFF_HEADS, DIFF_HEAD_DIM
    f32 = jnp.float32
    qkv = h @ p['w_qkv']
    q = _partial_rope(qkv[..., :C].reshape(B, T, H, 2, d), cos, sin)
    k = _partial_rope(qkv[..., C:2 * C].reshape(B, T, H, 2, d), cos, sin)
    v = qkv[..., 2 * C:].reshape(B, T, H, 2 * d)
    lp = p['lambda'].astype(f32)
    lam = jnp.exp(jnp.sum(lp[0] * lp[1])) - jnp.exp(jnp.sum(lp[2] * lp[3])) + lambda_init
    scale = d ** -0.5

    def attend(qb):
        s = jnp.einsum('bqhcd,bkhcd->bhcqk', qb, k, preferred_element_type=f32) * scale
        pr = jax.nn.softmax(s, axis=-1)
        dmap = pr[:, :, 0] - lam * pr[:, :, 1]
        return jnp.einsum('bhqk,bkhe->bqhe', dmap.astype(v.dtype), v)

    o = _merge_blocks(lax.map(attend, _query_blocks(q)))
    o = _rmsnorm(o, p['subln'], eps=DIFF_EPS) * (1.0 - lambda_init)
    return o.reshape(B, T, C) @ p['w_o']


def _diff_lambda_init(layer):
    return 0.8 - 0.6 * math.exp(-0.3 * layer)


def _trunk(x, layers, final_norm):
    T = x.shape[1]
    cos_mla, sin_mla = _rope_tables(T, MLA_ROPE_DIM)
    cos_diff, sin_diff = _rope_tables(T, DIFF_ROT_DIM)
    v_first = None
    for i in range(DEPTH):
        p = layers[i]
        kind = i % N_MIXERS
        h = _rmsnorm(x, p['norm_mix'])
        if kind == 0:
            mixed, v_layer = _rwkv7_mixer(h, p, v_first)
            if v_first is None:
                v_first = v_layer
        elif kind == 1:
            mixed = _mla_mixer(h, p, cos_mla, sin_mla)
        else:
            mixed = _diff_mixer(h, p, cos_diff, sin_diff, _diff_lambda_init(i))
        x = x + mixed.astype(x.dtype)
        x = x + _swiglu(_rmsnorm(x, p['norm_ffn']), p['w_gu'], p['w_down']).astype(x.dtype)
    return _rmsnorm(x, final_norm)


def _normal(key, shape, scale):
    return scale * jax.random.normal(key, shape, jnp.float32)


def _dense(key, shape, fan_in, gain=1.0):
    return _normal(key, shape, gain * fan_in ** -0.5)


def _gain(key, shape):
    return 1.0 + _normal(key, shape, 0.02)


def _ffn_params(key, pre):
    k1, k2, k3 = jax.random.split(key, 3)
    return {pre + 'norm_ffn': _gain(k1, (D_MODEL,)),
            pre + 'w_gu': _dense(k2, (D_MODEL, 2 * FFN_HIDDEN), D_MODEL),
            pre + 'w_down': _dense(k3, (FFN_HIDDEN, D_MODEL), FFN_HIDDEN)}


def _rwkv_params(key, pre, value_residual):
    C = D_MODEL
    ks = iter(jax.random.split(key, 24))
    p = {pre + 'norm_mix': _gain(next(ks), (C,)),
         pre + 'mix': jax.random.uniform(next(ks), (6, C), jnp.float32),
         pre + 'w_rkv': _dense(next(ks), (3, C, C), C),
         pre + 'w_decay0': jax.random.uniform(next(ks), (2, C), jnp.float32, -3.0, 2.0),
         pre + 'w_decay1': _dense(next(ks), (2, C, RWKV_DECAY_LORA), C),
         pre + 'w_decay2': _dense(next(ks), (2, RWKV_DECAY_LORA, C), RWKV_DECAY_LORA, 0.5),
         pre + 'a0': _normal(next(ks), (2, C), 0.1),
         pre + 'a1': _dense(next(ks), (2, C, RWKV_A_LORA), C),
         pre + 'a2': _dense(next(ks), (2, RWKV_A_LORA, C), RWKV_A_LORA, 0.5)}
    if value_residual:
        p[pre + 'v0'] = _normal(next(ks), (C,), 0.1)
        p[pre + 'v1'] = _dense(next(ks), (C, RWKV_V_LORA), C)
        p[pre + 'v2'] = _dense(next(ks), (RWKV_V_LORA, C), RWKV_V_LORA, 0.5)
    p[pre + 'g1'] = _dense(next(ks), (C, RWKV_G_LORA), C)
    p[pre + 'g2'] = _dense(next(ks), (RWKV_G_LORA, C), RWKV_G_LORA)
    p[pre + 'k_k'] = 0.85 + _normal(next(ks), (C,), 0.02)
    p[pre + 'k_a'] = 1.0 + _normal(next(ks), (C,), 0.02)
    p[pre + 'r_k'] = _normal(next(ks), (RWKV_HEADS, RWKV_HEAD_DIM), 0.1)
    p[pre + 'lnx_w'] = _gain(next(ks), (C,))
    p[pre + 'lnx_b'] = _normal(next(ks), (C,), 0.02)
    p[pre + 'w_o'] = _dense(next(ks), (C, C), C)
    p.update(_ffn_params(next(ks), pre))
    return p


def _mla_params(key, pre):
    C = D_MODEL
    ks = iter(jax.random.split(key, 8))
    p = {pre + 'norm_mix': _gain(next(ks), (C,)),
         pre + 'w_in': _dense(next(ks), (C, MLA_Q_RANK + MLA_KV_RANK + MLA_ROPE_DIM), C),
         pre + 'q_norm': _gain(next(ks), (MLA_Q_RANK,)),
         pre + 'kv_norm': _gain(next(ks), (MLA_KV_RANK,)),
         pre + 'w_uq': _dense(next(ks), (MLA_Q_RANK, MLA_HEADS * (MLA_NOPE_DIM + MLA_ROPE_DIM)), MLA_Q_RANK),
         pre + 'w_ukv': _dense(next(ks), (MLA_KV_RANK, MLA_HEADS * (MLA_NOPE_DIM + MLA_V_DIM)), MLA_KV_RANK),
         pre + 'w_o': _dense(next(ks), (MLA_HEADS * MLA_V_DIM, C), MLA_HEADS * MLA_V_DIM)}
    p.update(_ffn_params(next(ks), pre))
    return p


def _diff_params(key, pre):
    C = D_MODEL
    ks = iter(jax.random.split(key, 6))
    p = {pre + 'norm_mix': _gain(next(ks), (C,)),
         pre + 'w_qkv': _dense(next(ks), (C, 3 * C), C),
         pre + 'lambda': _normal(next(ks), (4, DIFF_HEAD_DIM), 0.1),
         pre + 'subln': _gain(next(ks), (2 * DIFF_HEAD_DIM,)),
         pre + 'w_o': _dense(next(ks), (C, C), C)}
    p.update(_ffn_params(next(ks), pre))
    return p


def setup_inputs(seed: int = 0) -> dict:
    key = jax.random.key(seed)
    kxp, kxs, k0, k1, k2, k3, kf = jax.random.split(key, 7)
    inputs = {'x_prompt': jax.random.normal(kxp, (BATCH, SEQ, D_MODEL), jnp.float32),
              'x_sample': jax.random.normal(kxs, (DEC_BATCH, DEC_SEQ, D_MODEL), jnp.float32)}
    inputs.update(_rwkv_params(k0, 'l0_', False))
    inputs.update(_mla_params(k1, 'l1_'))
    inputs.update(_diff_params(k2, 'l2_'))
    inputs.update(_rwkv_params(k3, 'l3_', True))
    inputs['final_norm'] = _gain(kf, (D_MODEL,))
    return inputs


def reference(x_prompt, x_sample,
              l0_norm_mix, l0_mix, l0_w_rkv, l0_w_decay0, l0_w_decay1, l0_w_decay2, l0_a0, l0_a1, l0_a2,
              l0_g1, l0_g2, l0_k_k, l0_k_a, l0_r_k, l0_lnx_w, l0_lnx_b, l0_w_o, l0_norm_ffn, l0_w_gu, l0_w_down,
              l1_norm_mix, l1_w_in, l1_q_norm, l1_kv_norm, l1_w_uq, l1_w_ukv, l1_w_o, l1_norm_ffn, l1_w_gu, l1_w_down,
              l2_norm_mix, l2_w_qkv, l2_lambda, l2_subln, l2_w_o, l2_norm_ffn, l2_w_gu, l2_w_down,
              l3_norm_mix, l3_mix, l3_w_rkv, l3_w_decay0, l3_w_decay1, l3_w_decay2, l3_a0, l3_a1, l3_a2,
              l3_v0, l3_v1, l3_v2,
              l3_g1, l3_g2, l3_k_k, l3_k_a, l3_r_k, l3_lnx_w, l3_lnx_b, l3_w_o, l3_norm_ffn, l3_w_gu, l3_w_down,
              final_norm):
    layers = (
        dict(norm_mix=l0_norm_mix, mix=l0_mix, w_rkv=l0_w_rkv, w_decay0=l0_w_decay0, w_decay1=l0_w_decay1,
             w_decay2=l0_w_decay2, a0=l0_a0, a1=l0_a1, a2=l0_a2, g1=l0_g1, g2=l0_g2, k_k=l0_k_k, k_a=l0_k_a,
             r_k=l0_r_k, lnx_w=l0_lnx_w, lnx_b=l0_lnx_b, w_o=l0_w_o,
             norm_ffn=l0_norm_ffn, w_gu=l0_w_gu, w_down=l0_w_down),
        dict(norm_mix=l1_norm_mix, w_in=l1_w_in, q_norm=l1_q_norm, kv_norm=l1_kv_norm, w_uq=l1_w_uq,
             w_ukv=l1_w_ukv, w_o=l1_w_o, norm_ffn=l1_norm_ffn, w_gu=l1_w_gu, w_down=l1_w_down),
        dict(norm_mix=l2_norm_mix, w_qkv=l2_w_qkv, subln=l2_subln, w_o=l2_w_o,
             norm_ffn=l2_norm_ffn, w_gu=l2_w_gu, w_down=l2_w_down, **{'lambda': l2_lambda}),
        dict(norm_mix=l3_norm_mix, mix=l3_mix, w_rkv=l3_w_rkv, w_decay0=l3_w_decay0, w_decay1=l3_w_decay1,
             w_decay2=l3_w_decay2, a0=l3_a0, a1=l3_a1, a2=l3_a2, v0=l3_v0, v1=l3_v1, v2=l3_v2,
             g1=l3_g1, g2=l3_g2, k_k=l3_k_k, k_a=l3_k_a, r_k=l3_r_k, lnx_w=l3_lnx_w, lnx_b=l3_lnx_b, w_o=l3_w_o,
             norm_ffn=l3_norm_ffn, w_gu=l3_w_gu, w_down=l3_w_down),
    )
    y_prompt = _trunk(x_prompt, layers, final_norm)
    y_sample = _trunk(x_sample, layers, final_norm)
    return (y_prompt, y_sample)
```

```python
import functools
import math

import jax
import jax.numpy as jnp
from jax import lax
from jax.experimental import pallas as pl
from jax.experimental.pallas import tpu as pltpu

F32 = jnp.float32
BF16 = jnp.bfloat16

LANES = 128
VMEM_LIMIT = 56 * 1024 * 1024

RMS_EPS = 1e-6
ROPE_THETA = 500000.0
RWKV_HEAD_DIM = 64
RWKV_GN_EPS = 64e-5
WKV_CHUNK = 64
MLA_NOPE_DIM = 128
MLA_ROPE_DIM = 64
MLA_V_DIM = 128
MLA_QK_PAD = 256
DIFF_HEAD_DIM = 128
DIFF_ROT_DIM = DIFF_HEAD_DIM // 4
DIFF_EPS = 1e-5
DEPTH = 4

NN = (((1,), (0,)), ((), ()))
NT = (((1,), (1,)), ((), ()))
TN = (((0,), (0,)), ((), ()))


def _blk(n, pref, align):
    if n <= pref:
        return n
    d = (pref // align) * align
    while d >= align:
        if n % d == 0:
            return d
        d -= align
    raise ValueError(f"no block for {n} (pref {pref}, align {align})")


def _params(*sem):
    return pltpu.CompilerParams(dimension_semantics=sem, vmem_limit_bytes=VMEM_LIMIT)


def _pad_to(x, axis, mult):
    n = x.shape[axis]
    pad = (-n) % mult
    if pad == 0:
        return x
    widths = [(0, 0)] * x.ndim
    widths[axis] = (0, pad)
    return jnp.pad(x, widths)


def _rmsnorm_body(x_ref, g_ref, o_ref, *, eps):
    x = x_ref[...].astype(F32)
    y = x * lax.rsqrt(jnp.mean(x * x, axis=-1, keepdims=True) + eps)
    o_ref[...] = (y * g_ref[...]).astype(o_ref.dtype)


def _rmsnorm(x, g, *, eps=RMS_EPS, out_dtype=F32, width=None, col_block=0):
    M = x.shape[0]
    width = x.shape[1] if width is None else width
    tm = _blk(M, 512, 8)
    return pl.pallas_call(
        functools.partial(_rmsnorm_body, eps=eps),
        out_shape=jax.ShapeDtypeStruct((M, width), out_dtype),
        grid=(M // tm,),
        in_specs=[pl.BlockSpec((tm, width), lambda i: (i, col_block)),
                  pl.BlockSpec((1, width), lambda i: (0, 0))],
        out_specs=pl.BlockSpec((tm, width), lambda i: (i, 0)),
        compiler_params=_params("parallel"),
        name="rmsnorm",
    )(x, g.reshape(1, width).astype(F32))


def _rope_tables(T, dim):
    half = dim // 2
    inv_freq = 1.0 / (ROPE_THETA ** (jnp.arange(0, dim, 2, dtype=F32) / dim))
    ang = jnp.arange(T, dtype=F32)[:, None] * inv_freq[None, :]
    c, s = jnp.cos(ang), jnp.sin(ang)
    ones = jnp.ones((T, LANES - dim), F32)
    zeros = jnp.zeros((T, LANES - dim), F32)
    zh = jnp.zeros((T, half), F32)
    C = jnp.concatenate([c, c, ones], axis=1)
    S1 = jnp.concatenate([-s, zh, zeros], axis=1)
    S2 = jnp.concatenate([zh, s, zeros], axis=1)
    return C, S1, S2


def _rope_tile(t, c, s1, s2, half):
    return t * c + pltpu.roll(t, LANES - half, 1) * s1 + pltpu.roll(t, half, 1) * s2


def _rope_cols_body(x_ref, c_ref, s1_ref, s2_ref, o_ref, *, half):
    o_ref[...] = _rope_tile(x_ref[...].astype(F32), c_ref[...], s1_ref[...], s2_ref[...], half)


def _rope_cols(x, tables, T, col_block, half):
    M = x.shape[0]
    tm = _blk(T, 512, 8)
    nt = T // tm
    tab = pl.BlockSpec((tm, LANES), lambda i: (i % nt, 0))
    return pl.pallas_call(
        functools.partial(_rope_cols_body, half=half),
        out_shape=jax.ShapeDtypeStruct((M, LANES), F32),
        grid=(M // tm,),
        in_specs=[pl.BlockSpec((tm, LANES), lambda i: (i, col_block)), tab, tab, tab],
        out_specs=pl.BlockSpec((tm, LANES), lambda i: (i, 0)),
        compiler_params=_params("parallel"),
        name="rope_cols",
    )(x, *tables)


def _mm_body(*refs, n_extra, epilogue):
    x_ref, w_ref = refs[0], refs[1]
    extra = refs[2:2 + n_extra]
    o_ref = refs[2 + n_extra]
    acc = jnp.dot(x_ref[...], w_ref[...], preferred_element_type=F32)
    epilogue(acc, extra, o_ref)


def _epi_plain(acc, extra, o_ref, *, act):
    if act == "tanh":
        acc = jnp.tanh(acc)
    elif act == "sigmoid":
        acc = jax.nn.sigmoid(acc)
    o_ref[...] = acc.astype(o_ref.dtype)


def _epi_residual(acc, extra, o_ref):
    o_ref[...] = (extra[0][...] + acc).astype(o_ref.dtype)


def _epi_rope_second_tile(acc, extra, o_ref, *, half):
    c, s1, s2 = (e[...] for e in extra)
    o_ref[:, :LANES] = acc[:, :LANES].astype(o_ref.dtype)
    o_ref[:, LANES:] = _rope_tile(acc[:, LANES:], c, s1, s2, half).astype(o_ref.dtype)


def _epi_rope_tiles(acc, extra, o_ref, *, half, n_rope_blocks):
    j = pl.program_id(1)

    @pl.when(j < n_rope_blocks)
    def _():
        c, s1, s2 = (e[...] for e in extra)
        for t in range(acc.shape[1] // LANES):
            sl = slice(t * LANES, (t + 1) * LANES)
            o_ref[:, sl] = _rope_tile(acc[:, sl], c, s1, s2, half).astype(o_ref.dtype)

    @pl.when(j >= n_rope_blocks)
    def _():
        o_ref[...] = acc.astype(o_ref.dtype)


def _epi_add_second_tile(acc, extra, o_ref, *, n_add_blocks):
    j = pl.program_id(1)

    @pl.when(j < n_add_blocks)
    def _():
        o_ref[:, :LANES] = acc[:, :LANES].astype(o_ref.dtype)
        o_ref[:, LANES:] = (acc[:, LANES:] + extra[0][...]).astype(o_ref.dtype)

    @pl.when(j >= n_add_blocks)
    def _():
        o_ref[...] = acc.astype(o_ref.dtype)


def _matmul(x, w, *, out_dtype, epilogue=None, extra=(), extra_specs=(), tm=512, tn=512,
            x_col_block=0, name="matmul"):
    M = x.shape[0]
    K, N = w.shape
    tm = _blk(M, tm, 8)
    tn = _blk(N, tn, LANES)
    if epilogue is None:
        epilogue = functools.partial(_epi_plain, act=None)
    return pl.pallas_call(
        functools.partial(_mm_body, n_extra=len(extra), epilogue=epilogue),
        out_shape=jax.ShapeDtypeStruct((M, N), out_dtype),
        grid=(M // tm, N // tn),
        in_specs=[pl.BlockSpec((tm, K), lambda i, j: (i, x_col_block)),
                  pl.BlockSpec((K, tn), lambda i, j: (0, j))] + list(extra_specs),
        out_specs=pl.BlockSpec((tm, tn), lambda i, j: (i, j)),
        compiler_params=_params("parallel", "arbitrary"),
        name=name,
    )(x, w, *extra)


def _matmul_residual(x, w, res, *, tm=512, tn=512, name="matmul_residual"):
    M = x.shape[0]
    N = w.shape[1]
    tm_ = _blk(M, tm, 8)
    tn_ = _blk(N, tn, LANES)
    return _matmul(x, w, out_dtype=F32, epilogue=_epi_residual, extra=(res,),
                   extra_specs=(pl.BlockSpec((tm_, tn_), lambda i, j: (i, j)),), tm=tm, tn=tn, name=name)


def _table_specs(T, tm):
    nt = T // tm
    return tuple(pl.BlockSpec((tm, LANES), lambda i, j: (i % nt, 0)) for _ in range(3))


def _swiglu_body(x_ref, wg_ref, wu_ref, o_ref):
    x = x_ref[...]
    g = jnp.dot(x, wg_ref[...], preferred_element_type=F32)
    u = jnp.dot(x, wu_ref[...], preferred_element_type=F32)
    o_ref[...] = (g * jax.nn.sigmoid(g) * u).astype(o_ref.dtype)


def _swiglu_up(x, w_gu, hidden, *, tm=1024, tn=256):
    M, K = x.shape
    tm = _blk(M, tm, 8)
    tn = _blk(hidden, tn, LANES)
    nb = hidden // tn
    return pl.pallas_call(
        _swiglu_body,
        out_shape=jax.ShapeDtypeStruct((M, hidden), BF16),
        grid=(M // tm, nb),
        in_specs=[pl.BlockSpec((tm, K), lambda i, j: (i, 0)),
                  pl.BlockSpec((K, tn), lambda i, j: (0, j)),
                  pl.BlockSpec((K, tn), lambda i, j: (0, j + nb))],
        out_specs=pl.BlockSpec((tm, tn), lambda i, j: (i, j)),
        compiler_params=_params("parallel", "arbitrary"),
        name="swiglu_up",
    )(x, w_gu, w_gu)


def _flash_body(*refs, n_maps, dqk, scale, diff, lambda_init):
    if diff:
        q_ref, k_ref, v_ref, lam_ref, g_ref, o_ref, m_scr, l_scr, acc_scr = refs
    else:
        q_ref, k_ref, v_ref, o_ref, m_scr, l_scr, acc_scr = refs
    j = pl.program_id(3)

    @pl.when(j == 0)
    def _():
        m_scr[...] = jnp.full(m_scr.shape, -jnp.inf, F32)
        l_scr[...] = jnp.zeros(l_scr.shape, F32)
        acc_scr[...] = jnp.zeros(acc_scr.shape, F32)

    q = q_ref[0]
    k = k_ref[0]
    v = v_ref[0]
    for mp in range(n_maps):
        sl = slice(mp * dqk, (mp + 1) * dqk)
        s = lax.dot_general(q[:, sl], k[:, sl], NT, preferred_element_type=F32) * scale
        m_prev = m_scr[mp][:, :1]
        m_new = jnp.maximum(m_prev, jnp.max(s, axis=-1, keepdims=True))
        alpha = jnp.exp(m_prev - m_new)
        p = jnp.exp(s - m_new)
        l_new = alpha * l_scr[mp][:, :1] + jnp.sum(p, axis=-1, keepdims=True)
        acc_scr[mp] = alpha * acc_scr[mp] + jnp.dot(p.astype(v.dtype), v, preferred_element_type=F32)
        m_scr[mp] = jnp.broadcast_to(m_new, m_scr.shape[1:])
        l_scr[mp] = jnp.broadcast_to(l_new, l_scr.shape[1:])

    @pl.when(j == pl.num_programs(3) - 1)
    def _():
        if diff:
            lp = lam_ref[...]
            lam = (jnp.exp(jnp.sum(lp[0:1] * lp[1:2], axis=-1, keepdims=True))
                   - jnp.exp(jnp.sum(lp[2:3] * lp[3:4], axis=-1, keepdims=True)) + lambda_init)
            o = acc_scr[0] / l_scr[0][:, :1] - lam * (acc_scr[1] / l_scr[1][:, :1])
            y = o * lax.rsqrt(jnp.mean(o * o, axis=-1, keepdims=True) + DIFF_EPS) * g_ref[...]
            o_ref[0] = (y * (1.0 - lambda_init)).astype(o_ref.dtype)
        else:
            o_ref[0] = (acc_scr[0] / l_scr[0][:, :1]).astype(o_ref.dtype)


def _flash(qkv_q, qkv_k, qkv_v, *, heads, n_maps, dqk, dv, q_blk0, k_blk0, v_blk0, scale,
           diff_params=None, lambda_init=0.0, bq=512, bk=512):
    B, T = qkv_q.shape[0], qkv_q.shape[1]
    bq = _blk(T, bq, 8)
    bk = _blk(T, bk, LANES)
    wqk = n_maps * dqk
    diff = diff_params is not None
    in_specs = [pl.BlockSpec((1, bq, wqk), lambda b, h, i, j: (b, i, q_blk0 + h)),
                pl.BlockSpec((1, bk, wqk), lambda b, h, i, j: (b, j, k_blk0 + h)),
                pl.BlockSpec((1, bk, dv), lambda b, h, i, j: (b, j, v_blk0 + h))]
    args = [qkv_q, qkv_k, qkv_v]
    if diff:
        lam, subln = diff_params
        in_specs += [pl.BlockSpec(lam.shape, lambda b, h, i, j: (0, 0)),
                     pl.BlockSpec(subln.shape, lambda b, h, i, j: (0, 0))]
        args += [lam, subln]
    return pl.pallas_call(
        functools.partial(_flash_body, n_maps=n_maps, dqk=dqk, scale=scale, diff=diff,
                          lambda_init=lambda_init),
        out_shape=jax.ShapeDtypeStruct((B, T, heads * dv), BF16),
        grid=(B, heads, T // bq, T // bk),
        in_specs=in_specs,
        out_specs=pl.BlockSpec((1, bq, dv), lambda b, h, i, j: (b, i, h)),
        scratch_shapes=[pltpu.VMEM((n_maps, bq, LANES), F32),
                        pltpu.VMEM((n_maps, bq, LANES), F32),
                        pltpu.VMEM((n_maps, bq, dv), F32)],
        compiler_params=_params("parallel", "parallel", "parallel", "arbitrary"),
        name="flash_diff" if diff else "flash_mla",
    )(*args)


def _bdot(a, b, dims):
    return lax.dot_general(a.astype(BF16), b.astype(BF16), dims, preferred_element_type=F32)


def _split_dot(m_exact, x, n_parts):
    acc = None
    rem = x
    for _ in range(n_parts):
        part = rem.astype(BF16)
        rem = rem - part.astype(F32)
        d = lax.dot_general(m_exact, part, NN, preferred_element_type=F32)
        acc = d if acc is None else acc + d
    return acc


def _wkv_body(r_ref, k_ref, v_ref, zw_ref, za_ref, w0_ref, a0_ref, kk_ref, ka_ref, o_ref, s_scr,
              *, reverse, L, pairs):
    c = pl.program_id(2)

    @pl.when(c == 0)
    def _():
        s_scr[...] = jnp.zeros(s_scr.shape, F32)

    L2 = 2 * L
    ri = lax.broadcasted_iota(jnp.int32, (L, L), 0)
    ci = lax.broadcasted_iota(jnp.int32, (L, L), 1)
    tri = ((ri <= ci) if reverse else (ri >= ci)).astype(BF16)
    r2 = lax.broadcasted_iota(jnp.int32, (L2, L2), 0)
    c2 = lax.broadcasted_iota(jnp.int32, (L2, L2), 1)
    strict = (r2 < c2) if reverse else (r2 > c2)
    incl = (r2 <= c2) if reverse else (r2 >= c2)
    eye = jnp.where(r2 == c2, 1.0, 0.0).astype(F32)
    lane = lax.broadcasted_iota(jnp.int32, (L, LANES), 1)
    head0 = lane < RWKV_HEAD_DIM
    hr = lax.broadcasted_iota(jnp.int32, (LANES, LANES), 0) // RWKV_HEAD_DIM
    hc = lax.broadcasted_iota(jnp.int32, (LANES, LANES), 1) // RWKV_HEAD_DIM
    head_ones = jnp.where(hr == hc, 1.0, 0.0).astype(BF16)

    def stack(q):
        return jnp.concatenate([jnp.where(head0, q, 0.0), jnp.where(head0, 0.0, q)], axis=0).astype(BF16)

    for p in range(pairs):
        sl = slice(p * LANES, (p + 1) * LANES)
        r = r_ref[0, :, sl]
        k = k_ref[0, :, sl]
        v = v_ref[0, :, sl]
        lw = -math.exp(-0.5) * jax.nn.sigmoid(w0_ref[:, sl] + zw_ref[0, :, sl])
        asig = jax.nn.sigmoid(a0_ref[:, sl] + za_ref[0, :, sl])
        kkr = k * kk_ref[:, sl]
        n2 = _head_sum(kkr * kkr, head_ones)
        kk = kkr / jnp.maximum(jnp.sqrt(n2), 1e-12)
        kd = k * (1.0 + (asig - 1.0) * ka_ref[:, sl])
        a = -kk
        b = kk * asig

        cum = _split_dot(tri, lw, 3)
        tot = jnp.sum(lw, axis=0, keepdims=True)
        a_t = a * jnp.exp(cum - lw)
        r_t = r * jnp.exp(cum)
        pinv = jnp.exp(-cum)
        pend = jnp.exp(tot - cum)
        As, Rs = stack(a_t), stack(r_t)
        Bs, Ks = stack(b * pinv), stack(kd * pinv)
        Bh, Kh = stack(b * pend), stack(kd * pend)
        Vs = stack(v)

        m_ab = jnp.where(strict, lax.dot_general(As, Bs, NT, preferred_element_type=F32), 0.0)
        m_ak = jnp.where(strict, lax.dot_general(As, Ks, NT, preferred_element_type=F32), 0.0)
        m_rb = jnp.where(incl, lax.dot_general(Rs, Bs, NT, preferred_element_type=F32), 0.0)
        m_rk = jnp.where(incl, lax.dot_general(Rs, Ks, NT, preferred_element_type=F32), 0.0)

        s = 1
        inv = eye
        while s < L:
            sh = s.bit_length() - 1
            same_2s = (r2 >> (sh + 1)) == (c2 >> (sh + 1))
            same_s = (r2 >> sh) == (c2 >> sh)
            e = jnp.where(same_2s & jnp.logical_not(same_s), m_ab, 0.0)
            if s == 1:
                inv = eye + e
            else:
                inv = inv + _bdot(inv, _bdot(e, inv, NN), NN)
            s *= 2

        state = s_scr[p]
        sb = state.astype(BF16)
        a_s = lax.dot_general(As, sb, NT, preferred_element_type=F32)
        r_s = lax.dot_general(Rs, sb, NT, preferred_element_type=F32)
        w_s = a_s + _bdot(m_ak, Vs, NN)
        u_s = _bdot(inv, w_s, NN)
        o_s = r_s + _bdot(m_rb, u_s, NN) + _bdot(m_rk, Vs, NN)
        o_ref[0, :, sl] = o_s[:L] + o_s[L:]
        s_scr[p] = state * jnp.exp(tot) + _bdot(u_s, Bh, TN) + _bdot(Vs, Kh, TN)


def _head_sum(x, head_ones):
    hi = x.astype(BF16)
    lo = (x - hi.astype(F32)).astype(BF16)
    return (lax.dot_general(hi, head_ones, NN, preferred_element_type=F32)
            + lax.dot_general(lo, head_ones, NN, preferred_element_type=F32))


def _wkv(r, k, v, zw, za, w0, a0, k_k, k_a, *, reverse, pairs=4):
    B, T, C = r.shape
    L = WKV_CHUNK
    pairs = min(pairs, C // LANES)
    wl = pairs * LANES
    nc = T // L

    def tmap(b, g, c):
        return (b, (nc - 1 - c) if reverse else c, g)

    seq = pl.BlockSpec((1, L, wl), tmap)
    par = pl.BlockSpec((1, wl), lambda b, g, c: (0, g))
    return pl.pallas_call(
        functools.partial(_wkv_body, reverse=reverse, L=L, pairs=pairs),
        out_shape=jax.ShapeDtypeStruct((B, T, C), F32),
        grid=(B, C // wl, nc),
        in_specs=[seq, seq, seq, seq, seq, par, par, par, par],
        out_specs=seq,
        scratch_shapes=[pltpu.VMEM((pairs, LANES, LANES), F32)],
        compiler_params=_params("parallel", "parallel", "arbitrary"),
        name="wkv7_rev" if reverse else "wkv7_fwd",
    )(r, k, v, zw, za, w0, a0, k_k, k_a)


def _wkv_out_body(of_ref, ob_ref, r_ref, k_ref, v_ref, za0_ref, za1_ref, g_ref,
                  a00_ref, a01_ref, ka_ref, rk_ref, lw_ref, lb_ref, o_ref):
    hr = lax.broadcasted_iota(jnp.int32, (LANES, LANES), 0) // RWKV_HEAD_DIM
    hc = lax.broadcasted_iota(jnp.int32, (LANES, LANES), 1) // RWKV_HEAD_DIM
    head_ones = jnp.where(hr == hc, 1.0, 0.0).astype(BF16)
    inv_n = 1.0 / RWKV_HEAD_DIM
    for t in range(o_ref.shape[1] // LANES):
        sl = slice(t * LANES, (t + 1) * LANES)
        o = of_ref[:, sl] + ob_ref[:, sl]
        mu = _head_sum(o, head_ones) * inv_n
        d = o - mu
        var = _head_sum(d * d, head_ones) * inv_n
        on = d * lax.rsqrt(var + RWKV_GN_EPS) * lw_ref[:, sl] + lb_ref[:, sl]
        asum = jax.nn.sigmoid(a00_ref[:, sl] + za0_ref[:, sl]) + jax.nn.sigmoid(a01_ref[:, sl] + za1_ref[:, sl])
        ksum = k_ref[:, sl] * (2.0 + (asum - 2.0) * ka_ref[:, sl])
        bonus = _head_sum(r_ref[:, sl] * ksum * rk_ref[:, sl], head_ones) * v_ref[:, sl]
        o_ref[:, sl] = ((on + bonus) * g_ref[:, sl]).astype(o_ref.dtype)


def _wkv_out(o_f, o_b, r, k, v, za0, za1, g, a0, k_a, r_k, lnx_w, lnx_b):
    M, C = r.shape
    tm = _blk(M, 256, 8)
    tc = _blk(C, 512, LANES)
    seq = pl.BlockSpec((tm, tc), lambda i, j: (i, j))
    par = pl.BlockSpec((1, tc), lambda i, j: (0, j))
    row = lambda x: x.reshape(1, C).astype(F32)
    return pl.pallas_call(
        _wkv_out_body,
        out_shape=jax.ShapeDtypeStruct((M, C), BF16),
        grid=(M // tm, C // tc),
        in_specs=[seq] * 8 + [par] * 6,
        out_specs=seq,
        compiler_params=_params("parallel", "parallel"),
        name="wkv_out",
    )(o_f, o_b, r, k, v, za0, za1, g, row(a0[0]), row(a0[1]), row(k_a), row(r_k), row(lnx_w), row(lnx_b))


def _ffn(x2, p):
    h = _rmsnorm(x2, p["norm_ffn"], out_dtype=BF16)
    hidden = p["w_down"].shape[0]
    mid = _swiglu_up(h, p["w_gu"], hidden)
    return _matmul_residual(mid, p["w_down"], x2, tm=512, tn=256, name="ffn_down")


def _rwkv_layer(x, p, v_first):
    B, T, C = x.shape
    M = B * T
    x2 = x.reshape(M, C)
    h = _rmsnorm(x2, p["norm_mix"]).reshape(B, T, C)
    hp = jnp.pad(h, ((0, 0), (1, 1), (0, 0)))
    dx = 0.5 * (hp[:, :-2] + hp[:, 2:]) - h
    mix = p["mix"]
    xr, xw, xk, xv, xa, xg = ((h + dx * mix[i]).astype(BF16).reshape(M, C) for i in range(6))
    r = _matmul(xr, p["w_r"], out_dtype=F32, name="rwkv_r")
    k = _matmul(xk, p["w_k"], out_dtype=F32, name="rwkv_k")
    v = _matmul(xv, p["w_v"], out_dtype=F32, name="rwkv_v")
    tw = _matmul(xw, p["decay1"], out_dtype=BF16, epilogue=functools.partial(_epi_plain, act="tanh"),
                 name="rwkv_decay1")
    ta = _matmul(xa, p["a1"], out_dtype=BF16, name="rwkv_a1")
    zw = [_matmul(tw, p["decay2"][d], out_dtype=F32, x_col_block=d, name="rwkv_decay2") for d in range(2)]
    za = [_matmul(ta, p["a2"][d], out_dtype=F32, x_col_block=d, name="rwkv_a2") for d in range(2)]
    if v_first is not None:
        tv = _matmul(xv, p["v1"], out_dtype=BF16, name="rwkv_v1")
        vl = _matmul(tv, p["v2"], out_dtype=F32, name="rwkv_v2")
        v = v + (v_first - v) * jax.nn.sigmoid(p["v0"] + vl)
    tg = _matmul(xg, p["g1"], out_dtype=BF16, epilogue=functools.partial(_epi_plain, act="sigmoid"),
                 name="rwkv_g1")
    g = _matmul(tg, p["g2"], out_dtype=F32, name="rwkv_g2")
    seq = lambda t: t.reshape(B, T, C)
    row = lambda t: t.reshape(1, C).astype(F32)
    o_dir = [
        _wkv(seq(r), seq(k), seq(v), seq(zw[d]), seq(za[d]), row(p["w_decay0"][d]), row(p["a0"][d]),
             row(p["k_k"]), row(p["k_a"]), reverse=bool(d))
        for d in range(2)
    ]
    y = _wkv_out(o_dir[0].reshape(M, C), o_dir[1].reshape(M, C), r, k, v, za[0], za[1], g,
                 p["a0"], p["k_a"], p["r_k"], p["lnx_w"], p["lnx_b"])
    x2 = _matmul_residual(y, p["w_o"], x2, name="rwkv_o")
    return x2, v


def _mla_layer(x, p):
    B, T, C = x.shape
    M = B * T
    x2 = x.reshape(M, C)
    H = p["w_uq"].shape[1] // MLA_QK_PAD
    q_rank = p["q_norm"].shape[0]
    kv_rank = p["kv_norm"].shape[0]
    h = _rmsnorm(x2, p["norm_mix"], out_dtype=BF16)
    lat = _matmul(h, p["w_in"], out_dtype=F32, tn=256, name="mla_in")
    c_q = _rmsnorm(lat, p["q_norm"], out_dtype=BF16, width=q_rank, col_block=0)
    c_kv = _rmsnorm(lat, p["kv_norm"], out_dtype=BF16, width=kv_rank, col_block=q_rank // kv_rank)
    tables = _rope_tables(T, MLA_ROPE_DIM)
    half = MLA_ROPE_DIM // 2
    k_rope = _rope_cols(lat, tables, T, (q_rank + kv_rank) // LANES, half)
    tm = _blk(T, 512, 8)
    q = _matmul(c_q, p["w_uq"], out_dtype=BF16, tm=tm, tn=MLA_QK_PAD,
                epilogue=functools.partial(_epi_rope_second_tile, half=half),
                extra=tables, extra_specs=_table_specs(T, _blk(M, tm, 8)), name="mla_uq")
    tm_kv = _blk(M, 512, 8)
    kv = _matmul(c_kv, p["w_ukv"], out_dtype=BF16, tm=tm_kv, tn=MLA_QK_PAD,
                 epilogue=functools.partial(_epi_add_second_tile, n_add_blocks=H),
                 extra=(k_rope,), extra_specs=(pl.BlockSpec((tm_kv, LANES), lambda i, j: (i, 0)),),
                 name="mla_ukv")
    q3 = q.reshape(B, T, H * MLA_QK_PAD)
    kv3 = kv.reshape(B, T, kv.shape[1])
    o = _flash(q3, kv3, kv3, heads=H, n_maps=1, dqk=MLA_QK_PAD, dv=MLA_V_DIM,
               q_blk0=0, k_blk0=0, v_blk0=H * MLA_QK_PAD // MLA_V_DIM,
               scale=(MLA_NOPE_DIM + MLA_ROPE_DIM) ** -0.5)
    return _matmul_residual(o.reshape(M, H * MLA_V_DIM), p["w_o"], x2, name="mla_o")


def _diff_layer(x, p, lambda_init):
    B, T, C = x.shape
    M = B * T
    x2 = x.reshape(M, C)
    H = C // (2 * DIFF_HEAD_DIM)
    h = _rmsnorm(x2, p["norm_mix"], out_dtype=BF16)
    tables = _rope_tables(T, DIFF_ROT_DIM)
    tm = _blk(T, 512, 8)
    tn = _blk(C, 512, LANES)
    qkv = _matmul(h, p["w_qkv"], out_dtype=BF16, tm=tm, tn=tn,
                  epilogue=functools.partial(_epi_rope_tiles, half=DIFF_ROT_DIM // 2,
                                             n_rope_blocks=2 * C // tn),
                  extra=tables, extra_specs=_table_specs(T, _blk(M, tm, 8)), name="diff_qkv")
    qkv3 = qkv.reshape(B, T, 3 * C)
    o = _flash(qkv3, qkv3, qkv3, heads=H, n_maps=2, dqk=DIFF_HEAD_DIM, dv=2 * DIFF_HEAD_DIM,
               q_blk0=0, k_blk0=H, v_blk0=2 * H, scale=DIFF_HEAD_DIM ** -0.5,
               diff_params=(p["lambda"].astype(F32), p["subln"].reshape(1, -1).astype(F32)),
               lambda_init=lambda_init)
    return _matmul_residual(o.reshape(M, C), p["w_o"], x2, name="diff_o")


def _trunk(x, layers, final_norm):
    B, T, C = x.shape
    v_first = None
    for i, p in enumerate(layers):
        kind = p["kind"]
        if kind == "rwkv":
            x2, v_layer = _rwkv_layer(x, p, v_first)
            if v_first is None:
                v_first = v_layer
        elif kind == "mla":
            x2 = _mla_layer(x, p)
        else:
            x2 = _diff_layer(x, p, 0.8 - 0.6 * math.exp(-0.3 * i))
        x2 = _ffn(x2, p)
        x = x2.reshape(B, T, C)
    return _rmsnorm(x.reshape(B * T, C), final_norm).reshape(B, T, C)


def _prep_ffn(p):
    return {"norm_ffn": p["norm_ffn"], "w_gu": p["w_gu"].astype(BF16), "w_down": p["w_down"].astype(BF16)}


def _prep_rwkv(p):
    b = lambda w: w.astype(BF16)
    out = dict(kind="rwkv", norm_mix=p["norm_mix"], mix=p["mix"],
               w_r=b(p["w_rkv"][0]), w_k=b(p["w_rkv"][1]), w_v=b(p["w_rkv"][2]),
               w_decay0=p["w_decay0"], a0=p["a0"], k_k=p["k_k"], k_a=p["k_a"], r_k=p["r_k"],
               lnx_w=p["lnx_w"], lnx_b=p["lnx_b"], w_o=b(p["w_o"]))
    for name1, name2, key1, key2 in (("w_decay1", "w_decay2", "decay1", "decay2"), ("a1", "a2", "a1", "a2")):
        w1 = [_pad_to(p[name1][d], 1, LANES) for d in range(2)]
        out[key1] = b(jnp.concatenate(w1, axis=1))
        out[key2] = [b(_pad_to(p[name2][d], 0, LANES)) for d in range(2)]
    if "v0" in p:
        out["v0"] = p["v0"]
        out["v1"] = b(_pad_to(p["v1"], 1, LANES))
        out["v2"] = b(_pad_to(p["v2"], 0, LANES))
    out["g1"] = b(_pad_to(p["g1"], 1, LANES))
    out["g2"] = b(_pad_to(p["g2"], 0, LANES))
    out.update(_prep_ffn(p))
    return out


def _prep_mla(p):
    b = lambda w: w.astype(BF16)
    q_rank = p["q_norm"].shape[0]
    kv_rank = p["kv_norm"].shape[0]
    qk = MLA_NOPE_DIM + MLA_ROPE_DIM
    H = p["w_uq"].shape[1] // qk
    w_uq = _pad_to(p["w_uq"].reshape(q_rank, H, qk), 2, MLA_QK_PAD).reshape(q_rank, H * MLA_QK_PAD)
    w_ukv = p["w_ukv"].reshape(kv_rank, H, MLA_NOPE_DIM + MLA_V_DIM)
    w_k = _pad_to(w_ukv[:, :, :MLA_NOPE_DIM], 2, MLA_QK_PAD).reshape(kv_rank, H * MLA_QK_PAD)
    w_v = w_ukv[:, :, MLA_NOPE_DIM:].reshape(kv_rank, H * MLA_V_DIM)
    out = dict(kind="mla", norm_mix=p["norm_mix"], w_in=b(_pad_to(p["w_in"], 1, 256)),
               q_norm=p["q_norm"], kv_norm=p["kv_norm"], w_uq=b(w_uq),
               w_ukv=b(jnp.concatenate([w_k, w_v], axis=1)), w_o=b(p["w_o"]))
    out.update(_prep_ffn(p))
    return out


def _prep_diff(p):
    out = dict(kind="diff", norm_mix=p["norm_mix"], w_qkv=p["w_qkv"].astype(BF16), subln=p["subln"],
               w_o=p["w_o"].astype(BF16))
    out["lambda"] = p["lambda"]
    out.update(_prep_ffn(p))
    return out


def _forward(x_prompt, x_sample, raw_layers, final_norm):
    prep = {"rwkv": _prep_rwkv, "mla": _prep_mla, "diff": _prep_diff}
    kinds = ("rwkv", "mla", "diff")
    layers = [prep[kinds[i % 3]](p) for i, p in enumerate(raw_layers)]
    return (_trunk(x_prompt, layers, final_norm), _trunk(x_sample, layers, final_norm))


def kernel(x_prompt, x_sample, l0_norm_mix, l0_mix, l0_w_rkv, l0_w_decay0, l0_w_decay1, l0_w_decay2, l0_a0, l0_a1, l0_a2, l0_g1, l0_g2, l0_k_k, l0_k_a, l0_r_k, l0_lnx_w, l0_lnx_b, l0_w_o, l0_norm_ffn, l0_w_gu, l0_w_down, l1_norm_mix, l1_w_in, l1_q_norm, l1_kv_norm, l1_w_uq, l1_w_ukv, l1_w_o, l1_norm_ffn, l1_w_gu, l1_w_down, l2_norm_mix, l2_w_qkv, l2_lambda, l2_subln, l2_w_o, l2_norm_ffn, l2_w_gu, l2_w_down, l3_norm_mix, l3_mix, l3_w_rkv, l3_w_decay0, l3_w_decay1, l3_w_decay2, l3_a0, l3_a1, l3_a2, l3_v0, l3_v1, l3_v2, l3_g1, l3_g2, l3_k_k, l3_k_a, l3_r_k, l3_lnx_w, l3_lnx_b, l3_w_o, l3_norm_ffn, l3_w_gu, l3_w_down, final_norm):
    raw_layers = (
        dict(norm_mix=l0_norm_mix, mix=l0_mix, w_rkv=l0_w_rkv, w_decay0=l0_w_decay0, w_decay1=l0_w_decay1,
             w_decay2=l0_w_decay2, a0=l0_a0, a1=l0_a1, a2=l0_a2, g1=l0_g1, g2=l0_g2, k_k=l0_k_k, k_a=l0_k_a,
             r_k=l0_r_k, lnx_w=l0_lnx_w, lnx_b=l0_lnx_b, w_o=l0_w_o,
             norm_ffn=l0_norm_ffn, w_gu=l0_w_gu, w_down=l0_w_down),
        dict(norm_mix=l1_norm_mix, w_in=l1_w_in, q_norm=l1_q_norm, kv_norm=l1_kv_norm, w_uq=l1_w_uq,
             w_ukv=l1_w_ukv, w_o=l1_w_o, norm_ffn=l1_norm_ffn, w_gu=l1_w_gu, w_down=l1_w_down),
        dict(norm_mix=l2_norm_mix, w_qkv=l2_w_qkv, subln=l2_subln, w_o=l2_w_o,
             norm_ffn=l2_norm_ffn, w_gu=l2_w_gu, w_down=l2_w_down, **{"lambda": l2_lambda}),
        dict(norm_mix=l3_norm_mix, mix=l3_mix, w_rkv=l3_w_rkv, w_decay0=l3_w_decay0, w_decay1=l3_w_decay1,
             w_decay2=l3_w_decay2, a0=l3_a0, a1=l3_a1, a2=l3_a2, v0=l3_v0, v1=l3_v1, v2=l3_v2,
             g1=l3_g1, g2=l3_g2, k_k=l3_k_k, k_a=l3_k_a, r_k=l3_r_k, lnx_w=l3_lnx_w, lnx_b=l3_lnx_b, w_o=l3_w_o,
             norm_ffn=l3_norm_ffn, w_gu=l3_w_gu, w_down=l3_w_down),
    )
    return _forward(x_prompt, x_sample, raw_layers, final_norm)
```

```python
import functools
import math

import jax
import jax.numpy as jnp
from jax import lax
from jax.experimental import pallas as pl
from jax.experimental.pallas import tpu as pltpu

F32 = jnp.float32
BF16 = jnp.bfloat16

LANES = 128
VMEM_LIMIT = 56 * 1024 * 1024

RMS_EPS = 1e-6
ROPE_THETA = 500000.0
RWKV_HEAD_DIM = 64
RWKV_GN_EPS = 64e-5
WKV_CHUNK = 64
MLA_NOPE_DIM = 128
MLA_ROPE_DIM = 64
MLA_V_DIM = 128
MLA_QK_PAD = 256
DIFF_HEAD_DIM = 128
DIFF_ROT_DIM = DIFF_HEAD_DIM // 4
DIFF_EPS = 1e-5
DEPTH = 4

NN = (((1,), (0,)), ((), ()))
NT = (((1,), (1,)), ((), ()))
TN = (((0,), (0,)), ((), ()))


def _blk(n, pref, align):
    if n <= pref:
        return n
    d = (pref // align) * align
    while d >= align:
        if n % d == 0:
            return d
        d -= align
    raise ValueError(f"no block for {n} (pref {pref}, align {align})")


def _params(*sem):
    return pltpu.CompilerParams(dimension_semantics=sem, vmem_limit_bytes=VMEM_LIMIT)


def _pad_to(x, axis, mult):
    n = x.shape[axis]
    pad = (-n) % mult
    if pad == 0:
        return x
    widths = [(0, 0)] * x.ndim
    widths[axis] = (0, pad)
    return jnp.pad(x, widths)


def _rmsnorm_body(x_ref, g_ref, o_ref, *, eps):
    x = x_ref[...].astype(F32)
    y = x * lax.rsqrt(jnp.mean(x * x, axis=-1, keepdims=True) + eps)
    o_ref[...] = (y * g_ref[...]).astype(o_ref.dtype)


def _rmsnorm(x, g, *, eps=RMS_EPS, out_dtype=F32, width=None, col_block=0):
    M = x.shape[0]
    width = x.shape[1] if width is None else width
    tm = _blk(M, 512, 8)
    return pl.pallas_call(
        functools.partial(_rmsnorm_body, eps=eps),
        out_shape=jax.ShapeDtypeStruct((M, width), out_dtype),
        grid=(M // tm,),
        in_specs=[pl.BlockSpec((tm, width), lambda i: (i, col_block)),
                  pl.BlockSpec((1, width), lambda i: (0, 0))],
        out_specs=pl.BlockSpec((tm, width), lambda i: (i, 0)),
        compiler_params=_params("parallel"),
        name="rmsnorm",
    )(x, g.reshape(1, width).astype(F32))


def _rope_tables(T, dim):
    half = dim // 2
    inv_freq = 1.0 / (ROPE_THETA ** (jnp.arange(0, dim, 2, dtype=F32) / dim))
    ang = jnp.arange(T, dtype=F32)[:, None] * inv_freq[None, :]
    c, s = jnp.cos(ang), jnp.sin(ang)
    ones = jnp.ones((T, LANES - dim), F32)
    zeros = jnp.zeros((T, LANES - dim), F32)
    zh = jnp.zeros((T, half), F32)
    C = jnp.concatenate([c, c, ones], axis=1)
    S1 = jnp.concatenate([-s, zh, zeros], axis=1)
    S2 = jnp.concatenate([zh, s, zeros], axis=1)
    return C, S1, S2


def _rope_tile(t, c, s1, s2, half):
    return t * c + pltpu.roll(t, LANES - half, 1) * s1 + pltpu.roll(t, half, 1) * s2


def _rope_cols_body(x_ref, c_ref, s1_ref, s2_ref, o_ref, *, half):
    o_ref[...] = _rope_tile(x_ref[...].astype(F32), c_ref[...], s1_ref[...], s2_ref[...], half)


def _rope_cols(x, tables, T, col_block, half):
    M = x.shape[0]
    tm = _blk(T, 512, 8)
    nt = T // tm
    tab = pl.BlockSpec((tm, LANES), lambda i: (i % nt, 0))
    return pl.pallas_call(
        functools.partial(_rope_cols_body, half=half),
        out_shape=jax.ShapeDtypeStruct((M, LANES), F32),
        grid=(M // tm,),
        in_specs=[pl.BlockSpec((tm, LANES), lambda i: (i, col_block)), tab, tab, tab],
        out_specs=pl.BlockSpec((tm, LANES), lambda i: (i, 0)),
        compiler_params=_params("parallel"),
        name="rope_cols",
    )(x, *tables)


def _mm_body(*refs, n_extra, epilogue):
    x_ref, w_ref = refs[0], refs[1]
    extra = refs[2:2 + n_extra]
    o_ref = refs[2 + n_extra]
    acc = jnp.dot(x_ref[...], w_ref[...], preferred_element_type=F32)
    epilogue(acc, extra, o_ref)


def _epi_plain(acc, extra, o_ref, *, act):
    if act == "tanh":
        acc = jnp.tanh(acc)
    elif act == "sigmoid":
        acc = jax.nn.sigmoid(acc)
    o_ref[...] = acc.astype(o_ref.dtype)


def _epi_residual(acc, extra, o_ref):
    o_ref[...] = (extra[0][...] + acc).astype(o_ref.dtype)


def _epi_rope_second_tile(acc, extra, o_ref, *, half, qscale):
    c, s1, s2 = (e[...] for e in extra)
    acc = acc * qscale
    o_ref[:, :LANES] = acc[:, :LANES].astype(o_ref.dtype)
    o_ref[:, LANES:] = _rope_tile(acc[:, LANES:], c, s1, s2, half).astype(o_ref.dtype)


def _epi_rope_tiles(acc, extra, o_ref, *, half, n_rope_blocks, n_q_blocks, qscale):
    j = pl.program_id(1)

    @pl.when(j < n_rope_blocks)
    def _():
        c, s1, s2 = (e[...] for e in extra)
        scaled = acc * jnp.where(j < n_q_blocks, qscale, 1.0).astype(F32)
        for t in range(acc.shape[1] // LANES):
            sl = slice(t * LANES, (t + 1) * LANES)
            o_ref[:, sl] = _rope_tile(scaled[:, sl], c, s1, s2, half).astype(o_ref.dtype)

    @pl.when(j >= n_rope_blocks)
    def _():
        o_ref[...] = acc.astype(o_ref.dtype)


def _epi_add_second_tile(acc, extra, o_ref, *, n_add_blocks):
    j = pl.program_id(1)

    @pl.when(j < n_add_blocks)
    def _():
        o_ref[:, :LANES] = acc[:, :LANES].astype(o_ref.dtype)
        o_ref[:, LANES:] = (acc[:, LANES:] + extra[0][...]).astype(o_ref.dtype)

    @pl.when(j >= n_add_blocks)
    def _():
        o_ref[...] = acc.astype(o_ref.dtype)


def _matmul(x, w, *, out_dtype, epilogue=None, extra=(), extra_specs=(), tm=512, tn=512,
            x_col_block=0, name="matmul"):
    M = x.shape[0]
    K, N = w.shape
    tm = _blk(M, tm, 8)
    tn = _blk(N, tn, LANES)
    if epilogue is None:
        epilogue = functools.partial(_epi_plain, act=None)
    return pl.pallas_call(
        functools.partial(_mm_body, n_extra=len(extra), epilogue=epilogue),
        out_shape=jax.ShapeDtypeStruct((M, N), out_dtype),
        grid=(M // tm, N // tn),
        in_specs=[pl.BlockSpec((tm, K), lambda i, j: (i, x_col_block)),
                  pl.BlockSpec((K, tn), lambda i, j: (0, j))] + list(extra_specs),
        out_specs=pl.BlockSpec((tm, tn), lambda i, j: (i, j)),
        compiler_params=_params("parallel", "arbitrary"),
        name=name,
    )(x, w, *extra)


def _matmul_residual(x, w, res, *, tm=512, tn=512, name="matmul_residual"):
    M = x.shape[0]
    N = w.shape[1]
    tm_ = _blk(M, tm, 8)
    tn_ = _blk(N, tn, LANES)
    return _matmul(x, w, out_dtype=F32, epilogue=_epi_residual, extra=(res,),
                   extra_specs=(pl.BlockSpec((tm_, tn_), lambda i, j: (i, j)),), tm=tm, tn=tn, name=name)


def _table_specs(T, tm):
    nt = T // tm
    return tuple(pl.BlockSpec((tm, LANES), lambda i, j: (i % nt, 0)) for _ in range(3))


def _swiglu_body(x_ref, wg_ref, wu_ref, o_ref):
    x = x_ref[...]
    g = jnp.dot(x, wg_ref[...], preferred_element_type=F32)
    u = jnp.dot(x, wu_ref[...], preferred_element_type=F32)
    o_ref[...] = (g * jax.nn.sigmoid(g) * u).astype(o_ref.dtype)


def _swiglu_up(x, w_gu, hidden, *, tm=1024, tn=256):
    M, K = x.shape
    tm = _blk(M, tm, 8)
    tn = _blk(hidden, tn, LANES)
    nb = hidden // tn
    return pl.pallas_call(
        _swiglu_body,
        out_shape=jax.ShapeDtypeStruct((M, hidden), BF16),
        grid=(M // tm, nb),
        in_specs=[pl.BlockSpec((tm, K), lambda i, j: (i, 0)),
                  pl.BlockSpec((K, tn), lambda i, j: (0, j)),
                  pl.BlockSpec((K, tn), lambda i, j: (0, j + nb))],
        out_specs=pl.BlockSpec((tm, tn), lambda i, j: (i, j)),
        compiler_params=_params("parallel", "arbitrary"),
        name="swiglu_up",
    )(x, w_gu, w_gu)


def _flash_body(*refs, n_maps, dqk, dv, diff, lambda_init, bks):
    if diff:
        q_ref, k_ref, vt_ref, lam_ref, g_ref, o_ref, m_scr, acc_scr = refs
    else:
        q_ref, k_ref, vt_ref, o_ref, m_scr, acc_scr = refs
    j = pl.program_id(3)

    @pl.when(j == 0)
    def _():
        m_scr[...] = jnp.full(m_scr.shape, -jnp.inf, F32)
        acc_scr[...] = jnp.zeros(acc_scr.shape, F32)

    q = q_ref[0]
    maps = [slice(mp * dqk, (mp + 1) * dqk) for mp in range(n_maps)]
    m = [m_scr[mp] for mp in range(n_maps)]
    acc = [acc_scr[mp] for mp in range(n_maps)]
    n_sub = k_ref.shape[1] // bks
    st = []
    for sub in range(n_sub):
        k = k_ref[0, sub * bks:(sub + 1) * bks, :]
        st.append([lax.dot_general(k[:, sl], q[:, sl], NT, preferred_element_type=F32) for sl in maps])
    for sub in range(n_sub):
        vt = vt_ref[0, :, sub * bks:(sub + 1) * bks]
        for mp in range(n_maps):
            s_t = st[sub][mp]
            m_new = jnp.maximum(m[mp], jnp.max(s_t, axis=0, keepdims=True))
            alpha = jnp.exp2(m[mp] - m_new)
            p = jnp.exp2(s_t - m_new).astype(vt.dtype)
            acc[mp] = alpha * acc[mp] + jnp.dot(vt, p, preferred_element_type=F32)
            m[mp] = m_new
    for mp in range(n_maps):
        m_scr[mp] = m[mp]
        acc_scr[mp] = acc[mp]

    @pl.when(j == pl.num_programs(3) - 1)
    def _():
        o = [(a[:dv] / a[dv:dv + 1]).T for a in acc]
        if diff:
            lp = lam_ref[...]
            lam = (jnp.exp(jnp.sum(lp[0:1] * lp[1:2], axis=-1, keepdims=True))
                   - jnp.exp(jnp.sum(lp[2:3] * lp[3:4], axis=-1, keepdims=True)) + lambda_init)
            od = o[0] - lam * o[1]
            y = od * lax.rsqrt(jnp.mean(od * od, axis=-1, keepdims=True) + DIFF_EPS) * g_ref[...]
            o_ref[0] = (y * (1.0 - lambda_init)).astype(o_ref.dtype)
        else:
            o_ref[0] = o[0].astype(o_ref.dtype)


ONES_ROWS = 16


def _v_transposed(v, heads, dv):
    B, T = v.shape[0], v.shape[1]
    vt = jnp.swapaxes(v, 1, 2).reshape(B, heads, dv, T)
    ones = jnp.ones((B, heads, ONES_ROWS, T), v.dtype)
    return jnp.concatenate([vt, ones], axis=2).reshape(B, heads * (dv + ONES_ROWS), T)


def _flash(q_arr, k_arr, v_arr, *, heads, n_maps, dqk, dv, q_blk0, k_blk0,
           diff_params=None, lambda_init=0.0, bq=512, bk=2048, bks=512):
    B, T = q_arr.shape[0], q_arr.shape[1]
    bq = _blk(T, bq, LANES)
    bk = _blk(T, bk, LANES)
    bks = _blk(bk, bks, LANES)
    wqk = n_maps * dqk
    dve = dv + ONES_ROWS
    diff = diff_params is not None
    in_specs = [pl.BlockSpec((1, bq, wqk), lambda b, h, i, j: (b, i, q_blk0 + h)),
                pl.BlockSpec((1, bk, wqk), lambda b, h, i, j: (b, j, k_blk0 + h)),
                pl.BlockSpec((1, dve, bk), lambda b, h, i, j: (b, h, j))]
    args = [q_arr, k_arr, _v_transposed(v_arr, heads, dv)]
    if diff:
        lam, subln = diff_params
        in_specs += [pl.BlockSpec(lam.shape, lambda b, h, i, j: (0, 0)),
                     pl.BlockSpec(subln.shape, lambda b, h, i, j: (0, 0))]
        args += [lam, subln]
    return pl.pallas_call(
        functools.partial(_flash_body, n_maps=n_maps, dqk=dqk, dv=dv, diff=diff,
                          lambda_init=lambda_init, bks=bks),
        out_shape=jax.ShapeDtypeStruct((B, T, heads * dv), BF16),
        grid=(B, heads, T // bq, T // bk),
        in_specs=in_specs,
        out_specs=pl.BlockSpec((1, bq, dv), lambda b, h, i, j: (b, i, h)),
        scratch_shapes=[pltpu.VMEM((n_maps, 1, bq), F32),
                        pltpu.VMEM((n_maps, dve, bq), F32)],
        compiler_params=_params("parallel", "parallel", "parallel", "arbitrary"),
        name="flash_diff" if diff else "flash_mla",
    )(*args)


def _split_dot(m_exact, x, n_parts):
    acc = None
    rem = x
    for _ in range(n_parts):
        part = rem.astype(BF16)
        rem = rem - part.astype(F32)
        d = lax.dot_general(m_exact, part, NN, preferred_element_type=F32)
        acc = d if acc is None else acc + d
    return acc


def _wkv_body(r_ref, k_ref, v_ref, zw_ref, za_ref, w0_ref, a0_ref, kk_ref, ka_ref, o_ref, s_scr,
              *, reverse, L, pairs):
    c = pl.program_id(2)

    @pl.when(c == 0)
    def _():
        s_scr[...] = jnp.zeros(s_scr.shape, F32)

    L2 = 2 * L
    ri = lax.broadcasted_iota(jnp.int32, (L, L), 0)
    ci = lax.broadcasted_iota(jnp.int32, (L, L), 1)
    tri = ((ri <= ci) if reverse else (ri >= ci)).astype(BF16)
    r2 = lax.broadcasted_iota(jnp.int32, (L2, L2), 0)
    c2 = lax.broadcasted_iota(jnp.int32, (L2, L2), 1)
    strict = (r2 < c2) if reverse else (r2 > c2)
    incl = (r2 <= c2) if reverse else (r2 >= c2)
    eye = jnp.where(r2 == c2, 1.0, 0.0).astype(F32)
    lane = lax.broadcasted_iota(jnp.int32, (L, LANES), 1)
    head0 = lane < RWKV_HEAD_DIM
    hr = lax.broadcasted_iota(jnp.int32, (LANES, LANES), 0) // RWKV_HEAD_DIM
    hc = lax.broadcasted_iota(jnp.int32, (LANES, LANES), 1) // RWKV_HEAD_DIM
    head_ones = jnp.where(hr == hc, 1.0, 0.0).astype(BF16)

    def stack(q):
        return jnp.concatenate([jnp.where(head0, q, 0.0), jnp.where(head0, 0.0, q)], axis=0).astype(BF16)

    r_all = r_ref[0]
    k_all = k_ref[0]
    lw_all = -math.exp(-0.5) * jax.nn.sigmoid(w0_ref[...] + zw_ref[0])
    asig = jax.nn.sigmoid(a0_ref[...] + za_ref[0])
    kkr = k_all * kk_ref[...]
    kkr2 = kkr * kkr
    n2 = jnp.concatenate([_head_sum(kkr2[:, p * LANES:(p + 1) * LANES], head_ones) for p in range(pairs)],
                         axis=1)
    kk = kkr / jnp.maximum(jnp.sqrt(n2), 1e-12)
    kd = k_all * (1.0 + (asig - 1.0) * ka_ref[...])
    b_all = kk * asig
    cum = _split_dot(tri, lw_all, 3)
    tot_all = jnp.sum(lw_all, axis=0, keepdims=True)
    at_all = -kk * jnp.exp(cum - lw_all)
    rt_all = r_all * jnp.exp(cum)
    pinv = jnp.exp(-cum)
    pend = jnp.exp(tot_all - cum)
    bi_all, ki_all = b_all * pinv, kd * pinv
    be_all, ke_all = b_all * pend, kd * pend
    decay_all = jnp.exp(tot_all)

    P = range(pairs)
    lanes = [slice(p * LANES, (p + 1) * LANES) for p in P]
    dotf = lambda a, b, dims: lax.dot_general(a, b, dims, preferred_element_type=F32)
    As = [stack(at_all[:, sl]) for sl in lanes]
    Bs = [stack(bi_all[:, sl]) for sl in lanes]
    m_ab = [jnp.where(strict, dotf(As[p], Bs[p], NT), 0.0) for p in P]

    inv = None
    s = 1
    while s < L:
        sh = s.bit_length() - 1
        off_diag = ((r2 >> (sh + 1)) == (c2 >> (sh + 1))) & ((r2 >> sh) != (c2 >> sh))
        e = [jnp.where(off_diag, m_ab[p], 0.0) for p in P]
        if s == 1:
            inv = [eye + e[p] for p in P]
        else:
            invb = [inv[p].astype(BF16) for p in P]
            t = [dotf(e[p].astype(BF16), invb[p], NN).astype(BF16) for p in P]
            inv = [inv[p] + dotf(invb[p], t[p], NN) for p in P]
        s *= 2

    Rs = [stack(rt_all[:, sl]) for sl in lanes]
    Ks = [stack(ki_all[:, sl]) for sl in lanes]
    Vs = [stack(v_ref[0, :, sl]) for sl in lanes]
    m_ak = [jnp.where(strict, dotf(As[p], Ks[p], NT), 0.0).astype(BF16) for p in P]
    m_rb = [jnp.where(incl, dotf(Rs[p], Bs[p], NT), 0.0).astype(BF16) for p in P]
    m_rk = [jnp.where(incl, dotf(Rs[p], Ks[p], NT), 0.0).astype(BF16) for p in P]
    state = [s_scr[p] for p in P]
    sb = [state[p].astype(BF16) for p in P]
    w_s = [dotf(As[p], sb[p], NT) + dotf(m_ak[p], Vs[p], NN) for p in P]
    u_s = [dotf(inv[p].astype(BF16), w_s[p].astype(BF16), NN).astype(BF16) for p in P]
    o_s = [dotf(Rs[p], sb[p], NT) + dotf(m_rb[p], u_s[p], NN) + dotf(m_rk[p], Vs[p], NN) for p in P]
    Bh = [stack(be_all[:, sl]) for sl in lanes]
    Kh = [stack(ke_all[:, sl]) for sl in lanes]
    new_state = [state[p] * decay_all[:, lanes[p]] + dotf(u_s[p], Bh[p], TN) + dotf(Vs[p], Kh[p], TN)
                 for p in P]
    o_ref[0] = jnp.concatenate([o[:L] + o[L:] for o in o_s], axis=1)
    s_scr[...] = jnp.stack(new_state)


def _head_sum(x, head_ones):
    hi = x.astype(BF16)
    lo = (x - hi.astype(F32)).astype(BF16)
    return (lax.dot_general(hi, head_ones, NN, preferred_element_type=F32)
            + lax.dot_general(lo, head_ones, NN, preferred_element_type=F32))


def _wkv(r, k, v, zw, za, w0, a0, k_k, k_a, *, reverse, pairs=8):
    B, T, C = r.shape
    L = WKV_CHUNK
    pairs = min(pairs, C // LANES)
    wl = pairs * LANES
    nc = T // L

    def tmap(b, g, c):
        return (b, (nc - 1 - c) if reverse else c, g)

    seq = pl.BlockSpec((1, L, wl), tmap)
    par = pl.BlockSpec((1, wl), lambda b, g, c: (0, g))
    return pl.pallas_call(
        functools.partial(_wkv_body, reverse=reverse, L=L, pairs=pairs),
        out_shape=jax.ShapeDtypeStruct((B, T, C), F32),
        grid=(B, C // wl, nc),
        in_specs=[seq, seq, seq, seq, seq, par, par, par, par],
        out_specs=seq,
        scratch_shapes=[pltpu.VMEM((pairs, LANES, LANES), F32)],
        compiler_params=_params("parallel", "parallel", "arbitrary"),
        name="wkv7_rev" if reverse else "wkv7_fwd",
    )(r, k, v, zw, za, w0, a0, k_k, k_a)


def _wkv_out_body(of_ref, ob_ref, r_ref, k_ref, v_ref, za0_ref, za1_ref, g_ref,
                  a00_ref, a01_ref, ka_ref, rk_ref, lw_ref, lb_ref, o_ref):
    hr = lax.broadcasted_iota(jnp.int32, (LANES, LANES), 0) // RWKV_HEAD_DIM
    hc = lax.broadcasted_iota(jnp.int32, (LANES, LANES), 1) // RWKV_HEAD_DIM
    head_ones = jnp.where(hr == hc, 1.0, 0.0).astype(BF16)
    inv_n = 1.0 / RWKV_HEAD_DIM
    for t in range(o_ref.shape[1] // LANES):
        sl = slice(t * LANES, (t + 1) * LANES)
        o = of_ref[:, sl] + ob_ref[:, sl]
        mu = _head_sum(o, head_ones) * inv_n
        d = o - mu
        var = _head_sum(d * d, head_ones) * inv_n
        on = d * lax.rsqrt(var + RWKV_GN_EPS) * lw_ref[:, sl] + lb_ref[:, sl]
        asum = jax.nn.sigmoid(a00_ref[:, sl] + za0_ref[:, sl]) + jax.nn.sigmoid(a01_ref[:, sl] + za1_ref[:, sl])
        ksum = k_ref[:, sl] * (2.0 + (asum - 2.0) * ka_ref[:, sl])
        bonus = _head_sum(r_ref[:, sl] * ksum * rk_ref[:, sl], head_ones) * v_ref[:, sl]
        o_ref[:, sl] = ((on + bonus) * g_ref[:, sl]).astype(o_ref.dtype)


def _wkv_out(o_f, o_b, r, k, v, za0, za1, g, a0, k_a, r_k, lnx_w, lnx_b):
    M, C = r.shape
    tm = _blk(M, 256, 8)
    tc = _blk(C, 512, LANES)
    seq = pl.BlockSpec((tm, tc), lambda i, j: (i, j))
    par = pl.BlockSpec((1, tc), lambda i, j: (0, j))
    row = lambda x: x.reshape(1, C).astype(F32)
    return pl.pallas_call(
        _wkv_out_body,
        out_shape=jax.ShapeDtypeStruct((M, C), BF16),
        grid=(M // tm, C // tc),
        in_specs=[seq] * 8 + [par] * 6,
        out_specs=seq,
        compiler_params=_params("parallel", "parallel"),
        name="wkv_out",
    )(o_f, o_b, r, k, v, za0, za1, g, row(a0[0]), row(a0[1]), row(k_a), row(r_k), row(lnx_w), row(lnx_b))


def _ffn(x2, p):
    h = _rmsnorm(x2, p["norm_ffn"], out_dtype=BF16)
    hidden = p["w_down"].shape[0]
    mid = _swiglu_up(h, p["w_gu"], hidden)
    return _matmul_residual(mid, p["w_down"], x2, tm=512, tn=256, name="ffn_down")


def _rwkv_layer(x, p, v_first):
    B, T, C = x.shape
    M = B * T
    x2 = x.reshape(M, C)
    h = _rmsnorm(x2, p["norm_mix"]).reshape(B, T, C)
    hp = jnp.pad(h, ((0, 0), (1, 1), (0, 0)))
    dx = 0.5 * (hp[:, :-2] + hp[:, 2:]) - h
    mix = p["mix"]
    xr, xw, xk, xv, xa, xg = ((h + dx * mix[i]).astype(BF16).reshape(M, C) for i in range(6))
    r = _matmul(xr, p["w_r"], out_dtype=F32, name="rwkv_r")
    k = _matmul(xk, p["w_k"], out_dtype=F32, name="rwkv_k")
    v = _matmul(xv, p["w_v"], out_dtype=F32, name="rwkv_v")
    tw = _matmul(xw, p["decay1"], out_dtype=BF16, epilogue=functools.partial(_epi_plain, act="tanh"),
                 name="rwkv_decay1")
    ta = _matmul(xa, p["a1"], out_dtype=BF16, name="rwkv_a1")
    zw = [_matmul(tw, p["decay2"][d], out_dtype=F32, x_col_block=d, name="rwkv_decay2") for d in range(2)]
    za = [_matmul(ta, p["a2"][d], out_dtype=F32, x_col_block=d, name="rwkv_a2") for d in range(2)]
    if v_first is not None:
        tv = _matmul(xv, p["v1"], out_dtype=BF16, name="rwkv_v1")
        vl = _matmul(tv, p["v2"], out_dtype=F32, name="rwkv_v2")
        v = v + (v_first - v) * jax.nn.sigmoid(p["v0"] + vl)
    tg = _matmul(xg, p["g1"], out_dtype=BF16, epilogue=functools.partial(_epi_plain, act="sigmoid"),
                 name="rwkv_g1")
    g = _matmul(tg, p["g2"], out_dtype=F32, name="rwkv_g2")
    seq = lambda t: t.reshape(B, T, C)
    row = lambda t: t.reshape(1, C).astype(F32)
    o_dir = [
        _wkv(seq(r), seq(k), seq(v), seq(zw[d]), seq(za[d]), row(p["w_decay0"][d]), row(p["a0"][d]),
             row(p["k_k"]), row(p["k_a"]), reverse=bool(d))
        for d in range(2)
    ]
    y = _wkv_out(o_dir[0].reshape(M, C), o_dir[1].reshape(M, C), r, k, v, za[0], za[1], g,
                 p["a0"], p["k_a"], p["r_k"], p["lnx_w"], p["lnx_b"])
    x2 = _matmul_residual(y, p["w_o"], x2, name="rwkv_o")
    return x2, v


def _mla_layer(x, p):
    B, T, C = x.shape
    M = B * T
    x2 = x.reshape(M, C)
    H = p["w_uq"].shape[1] // MLA_QK_PAD
    q_rank = p["q_norm"].shape[0]
    kv_rank = p["kv_norm"].shape[0]
    h = _rmsnorm(x2, p["norm_mix"], out_dtype=BF16)
    lat = _matmul(h, p["w_in"], out_dtype=F32, tn=256, name="mla_in")
    c_q = _rmsnorm(lat, p["q_norm"], out_dtype=BF16, width=q_rank, col_block=0)
    c_kv = _rmsnorm(lat, p["kv_norm"], out_dtype=BF16, width=kv_rank, col_block=q_rank // kv_rank)
    tables = _rope_tables(T, MLA_ROPE_DIM)
    half = MLA_ROPE_DIM // 2
    k_rope = _rope_cols(lat, tables, T, (q_rank + kv_rank) // LANES, half)
    tm = _blk(T, 512, 8)
    qscale = (MLA_NOPE_DIM + MLA_ROPE_DIM) ** -0.5 * math.log2(math.e)
    q = _matmul(c_q, p["w_uq"], out_dtype=BF16, tm=tm, tn=MLA_QK_PAD,
                epilogue=functools.partial(_epi_rope_second_tile, half=half, qscale=qscale),
                extra=tables, extra_specs=_table_specs(T, _blk(M, tm, 8)), name="mla_uq")
    tm_kv = _blk(M, 512, 8)
    kv = _matmul(c_kv, p["w_ukv"], out_dtype=BF16, tm=tm_kv, tn=MLA_QK_PAD,
                 epilogue=functools.partial(_epi_add_second_tile, n_add_blocks=H),
                 extra=(k_rope,), extra_specs=(pl.BlockSpec((tm_kv, LANES), lambda i, j: (i, 0)),),
                 name="mla_ukv")
    q3 = q.reshape(B, T, H * MLA_QK_PAD)
    kv3 = kv.reshape(B, T, kv.shape[1])
    o = _flash(q3, kv3, kv3[:, :, H * MLA_QK_PAD:], heads=H, n_maps=1, dqk=MLA_QK_PAD, dv=MLA_V_DIM,
               q_blk0=0, k_blk0=0)
    return _matmul_residual(o.reshape(M, H * MLA_V_DIM), p["w_o"], x2, name="mla_o")


def _diff_layer(x, p, lambda_init):
    B, T, C = x.shape
    M = B * T
    x2 = x.reshape(M, C)
    H = C // (2 * DIFF_HEAD_DIM)
    h = _rmsnorm(x2, p["norm_mix"], out_dtype=BF16)
    tables = _rope_tables(T, DIFF_ROT_DIM)
    tm = _blk(T, 512, 8)
    tn = _blk(C, 512, LANES)
    qkv = _matmul(h, p["w_qkv"], out_dtype=BF16, tm=tm, tn=tn,
                  epilogue=functools.partial(_epi_rope_tiles, half=DIFF_ROT_DIM // 2,
                                             n_rope_blocks=2 * C // tn, n_q_blocks=C // tn,
                                             qscale=DIFF_HEAD_DIM ** -0.5 * math.log2(math.e)),
                  extra=tables, extra_specs=_table_specs(T, _blk(M, tm, 8)), name="diff_qkv")
    qkv3 = qkv.reshape(B, T, 3 * C)
    o = _flash(qkv3, qkv3, qkv3[:, :, 2 * C:], heads=H, n_maps=2, dqk=DIFF_HEAD_DIM,
               dv=2 * DIFF_HEAD_DIM, q_blk0=0, k_blk0=H,
               diff_params=(p["lambda"].astype(F32), p["subln"].reshape(1, -1).astype(F32)),
               lambda_init=lambda_init)
    return _matmul_residual(o.reshape(M, C), p["w_o"], x2, name="diff_o")


def _trunk(x, layers, final_norm):
    B, T, C = x.shape
    v_first = None
    for i, p in enumerate(layers):
        kind = p["kind"]
        if kind == "rwkv":
            x2, v_layer = _rwkv_layer(x, p, v_first)
            if v_first is None:
                v_first = v_layer
        elif kind == "mla":
            x2 = _mla_layer(x, p)
        else:
            x2 = _diff_layer(x, p, 0.8 - 0.6 * math.exp(-0.3 * i))
        x2 = _ffn(x2, p)
        x = x2.reshape(B, T, C)
    return _rmsnorm(x.reshape(B * T, C), final_norm).reshape(B, T, C)


def _prep_ffn(p):
    return {"norm_ffn": p["norm_ffn"], "w_gu": p["w_gu"].astype(BF16), "w_down": p["w_down"].astype(BF16)}


def _prep_rwkv(p):
    b = lambda w: w.astype(BF16)
    out = dict(kind="rwkv", norm_mix=p["norm_mix"], mix=p["mix"],
               w_r=b(p["w_rkv"][0]), w_k=b(p["w_rkv"][1]), w_v=b(p["w_rkv"][2]),
               w_decay0=p["w_decay0"], a0=p["a0"], k_k=p["k_k"], k_a=p["k_a"], r_k=p["r_k"],
               lnx_w=p["lnx_w"], lnx_b=p["lnx_b"], w_o=b(p["w_o"]))
    for name1, name2, key1, key2 in (("w_decay1", "w_decay2", "decay1", "decay2"), ("a1", "a2", "a1", "a2")):
        w1 = [_pad_to(p[name1][d], 1, LANES) for d in range(2)]
        out[key1] = b(jnp.concatenate(w1, axis=1))
        out[key2] = [b(_pad_to(p[name2][d], 0, LANES)) for d in range(2)]
    if "v0" in p:
        out["v0"] = p["v0"]
        out["v1"] = b(_pad_to(p["v1"], 1, LANES))
        out["v2"] = b(_pad_to(p["v2"], 0, LANES))
    out["g1"] = b(_pad_to(p["g1"], 1, LANES))
    out["g2"] = b(_pad_to(p["g2"], 0, LANES))
    out.update(_prep_ffn(p))
    return out


def _prep_mla(p):
    b = lambda w: w.astype(BF16)
    q_rank = p["q_norm"].shape[0]
    kv_rank = p["kv_norm"].shape[0]
    qk = MLA_NOPE_DIM + MLA_ROPE_DIM
    H = p["w_uq"].shape[1] // qk
    w_uq = _pad_to(p["w_uq"].reshape(q_rank, H, qk), 2, MLA_QK_PAD).reshape(q_rank, H * MLA_QK_PAD)
    w_ukv = p["w_ukv"].reshape(kv_rank, H, MLA_NOPE_DIM + MLA_V_DIM)
    w_k = _pad_to(w_ukv[:, :, :MLA_NOPE_DIM], 2, MLA_QK_PAD).reshape(kv_rank, H * MLA_QK_PAD)
    w_v = w_ukv[:, :, MLA_NOPE_DIM:].reshape(kv_rank, H * MLA_V_DIM)
    out = dict(kind="mla", norm_mix=p["norm_mix"], w_in=b(_pad_to(p["w_in"], 1, 256)),
               q_norm=p["q_norm"], kv_norm=p["kv_norm"], w_uq=b(w_uq),
               w_ukv=b(jnp.concatenate([w_k, w_v], axis=1)), w_o=b(p["w_o"]))
    out.update(_prep_ffn(p))
    return out


def _prep_diff(p):
    out = dict(kind="diff", norm_mix=p["norm_mix"], w_qkv=p["w_qkv"].astype(BF16), subln=p["subln"],
               w_o=p["w_o"].astype(BF16))
    out["lambda"] = p["lambda"]
    out.update(_prep_ffn(p))
    return out


def _forward(x_prompt, x_sample, raw_layers, final_norm):
    prep = {"rwkv": _prep_rwkv, "mla": _prep_mla, "diff": _prep_diff}
    kinds = ("rwkv", "mla", "diff")
    layers = [prep[kinds[i % 3]](p) for i, p in enumerate(raw_layers)]
    return (_trunk(x_prompt, layers, final_norm), _trunk(x_sample, layers, final_norm))


def kernel(x_prompt, x_sample, l0_norm_mix, l0_mix, l0_w_rkv, l0_w_decay0, l0_w_decay1, l0_w_decay2, l0_a0, l0_a1, l0_a2, l0_g1, l0_g2, l0_k_k, l0_k_a, l0_r_k, l0_lnx_w, l0_lnx_b, l0_w_o, l0_norm_ffn, l0_w_gu, l0_w_down, l1_norm_mix, l1_w_in, l1_q_norm, l1_kv_norm, l1_w_uq, l1_w_ukv, l1_w_o, l1_norm_ffn, l1_w_gu, l1_w_down, l2_norm_mix, l2_w_qkv, l2_lambda, l2_subln, l2_w_o, l2_norm_ffn, l2_w_gu, l2_w_down, l3_norm_mix, l3_mix, l3_w_rkv, l3_w_decay0, l3_w_decay1, l3_w_decay2, l3_a0, l3_a1, l3_a2, l3_v0, l3_v1, l3_v2, l3_g1, l3_g2, l3_k_k, l3_k_a, l3_r_k, l3_lnx_w, l3_lnx_b, l3_w_o, l3_norm_ffn, l3_w_gu, l3_w_down, final_norm):
    raw_layers = (
        dict(norm_mix=l0_norm_mix, mix=l0_mix, w_rkv=l0_w_rkv, w_decay0=l0_w_decay0, w_decay1=l0_w_decay1,
             w_decay2=l0_w_decay2, a0=l0_a0, a1=l0_a1, a2=l0_a2, g1=l0_g1, g2=l0_g2, k_k=l0_k_k, k_a=l0_k_a,
             r_k=l0_r_k, lnx_w=l0_lnx_w, lnx_b=l0_lnx_b, w_o=l0_w_o,
             norm_ffn=l0_norm_ffn, w_gu=l0_w_gu, w_down=l0_w_down),
        dict(norm_mix=l1_norm_mix, w_in=l1_w_in, q_norm=l1_q_norm, kv_norm=l1_kv_norm, w_uq=l1_w_uq,
             w_ukv=l1_w_ukv, w_o=l1_w_o, norm_ffn=l1_norm_ffn, w_gu=l1_w_gu, w_down=l1_w_down),
        dict(norm_mix=l2_norm_mix, w_qkv=l2_w_qkv, subln=l2_subln, w_o=l2_w_o,
             norm_ffn=l2_norm_ffn, w_gu=l2_w_gu, w_down=l2_w_down, **{"lambda": l2_lambda}),
        dict(norm_mix=l3_norm_mix, mix=l3_mix, w_rkv=l3_w_rkv, w_decay0=l3_w_decay0, w_decay1=l3_w_decay1,
             w_decay2=l3_w_decay2, a0=l3_a0, a1=l3_a1, a2=l3_a2, v0=l3_v0, v1=l3_v1, v2=l3_v2,
             g1=l3_g1, g2=l3_g2, k_k=l3_k_k, k_a=l3_k_a, r_k=l3_r_k, lnx_w=l3_lnx_w, lnx_b=l3_lnx_b, w_o=l3_w_o,
             norm_ffn=l3_norm_ffn, w_gu=l3_w_gu, w_down=l3_w_down),
    )
    return _forward(x_prompt, x_sample, raw_layers, final_norm)
```

```python
import functools
import math

import jax
import jax.numpy as jnp
from jax import lax
from jax.experimental import pallas as pl
from jax.experimental.pallas import tpu as pltpu

F32 = jnp.float32
BF16 = jnp.bfloat16

LANES = 128
VMEM_LIMIT = 56 * 1024 * 1024

RMS_EPS = 1e-6
ROPE_THETA = 500000.0
RWKV_HEAD_DIM = 64
RWKV_GN_EPS = 64e-5
WKV_CHUNK = 64
MLA_NOPE_DIM = 128
MLA_ROPE_DIM = 64
MLA_V_DIM = 128
MLA_QK_PAD = 256
DIFF_HEAD_DIM = 128
DIFF_ROT_DIM = DIFF_HEAD_DIM // 4
DIFF_EPS = 1e-5
DEPTH = 4

NN = (((1,), (0,)), ((), ()))
NT = (((1,), (1,)), ((), ()))
TN = (((0,), (0,)), ((), ()))


def _blk(n, pref, align):
    if n <= pref:
        return n
    d = (pref // align) * align
    while d >= align:
        if n % d == 0:
            return d
        d -= align
    raise ValueError(f"no block for {n} (pref {pref}, align {align})")


def _params(*sem):
    return pltpu.CompilerParams(dimension_semantics=sem, vmem_limit_bytes=VMEM_LIMIT)


def _pad_to(x, axis, mult):
    n = x.shape[axis]
    pad = (-n) % mult
    if pad == 0:
        return x
    widths = [(0, 0)] * x.ndim
    widths[axis] = (0, pad)
    return jnp.pad(x, widths)


def _rmsnorm_body(x_ref, g_ref, o_ref, *, eps):
    x = x_ref[...].astype(F32)
    y = x * lax.rsqrt(jnp.mean(x * x, axis=-1, keepdims=True) + eps)
    o_ref[...] = (y * g_ref[...]).astype(o_ref.dtype)


def _rmsnorm(x, g, *, eps=RMS_EPS, out_dtype=F32, width=None, col_block=0):
    M = x.shape[0]
    width = x.shape[1] if width is None else width
    tm = _blk(M, 512, 8)
    return pl.pallas_call(
        functools.partial(_rmsnorm_body, eps=eps),
        out_shape=jax.ShapeDtypeStruct((M, width), out_dtype),
        grid=(M // tm,),
        in_specs=[pl.BlockSpec((tm, width), lambda i: (i, col_block)),
                  pl.BlockSpec((1, width), lambda i: (0, 0))],
        out_specs=pl.BlockSpec((tm, width), lambda i: (i, 0)),
        compiler_params=_params("parallel"),
        name="rmsnorm",
    )(x, g.reshape(1, width).astype(F32))


def _rope_tables(T, dim):
    half = dim // 2
    inv_freq = 1.0 / (ROPE_THETA ** (jnp.arange(0, dim, 2, dtype=F32) / dim))
    ang = jnp.arange(T, dtype=F32)[:, None] * inv_freq[None, :]
    c, s = jnp.cos(ang), jnp.sin(ang)
    ones = jnp.ones((T, LANES - dim), F32)
    zeros = jnp.zeros((T, LANES - dim), F32)
    zh = jnp.zeros((T, half), F32)
    C = jnp.concatenate([c, c, ones], axis=1)
    S1 = jnp.concatenate([-s, zh, zeros], axis=1)
    S2 = jnp.concatenate([zh, s, zeros], axis=1)
    return C, S1, S2


def _rope_tile(t, c, s1, s2, half):
    return t * c + pltpu.roll(t, LANES - half, 1) * s1 + pltpu.roll(t, half, 1) * s2


def _rope_cols_body(x_ref, c_ref, s1_ref, s2_ref, o_ref, *, half):
    o_ref[...] = _rope_tile(x_ref[...].astype(F32), c_ref[...], s1_ref[...], s2_ref[...], half)


def _rope_cols(x, tables, T, col_block, half):
    M = x.shape[0]
    tm = _blk(T, 512, 8)
    nt = T // tm
    tab = pl.BlockSpec((tm, LANES), lambda i: (i % nt, 0))
    return pl.pallas_call(
        functools.partial(_rope_cols_body, half=half),
        out_shape=jax.ShapeDtypeStruct((M, LANES), F32),
        grid=(M // tm,),
        in_specs=[pl.BlockSpec((tm, LANES), lambda i: (i, col_block)), tab, tab, tab],
        out_specs=pl.BlockSpec((tm, LANES), lambda i: (i, 0)),
        compiler_params=_params("parallel"),
        name="rope_cols",
    )(x, *tables)


def _mm_body(*refs, n_extra, epilogue):
    x_ref, w_ref = refs[0], refs[1]
    extra = refs[2:2 + n_extra]
    o_ref = refs[2 + n_extra]
    acc = jnp.dot(x_ref[...], w_ref[...], preferred_element_type=F32)
    epilogue(acc, extra, o_ref)


def _epi_plain(acc, extra, o_ref, *, act):
    if act == "tanh":
        acc = jnp.tanh(acc)
    elif act == "sigmoid":
        acc = jax.nn.sigmoid(acc)
    o_ref[...] = acc.astype(o_ref.dtype)


def _epi_residual(acc, extra, o_ref):
    o_ref[...] = (extra[0][...] + acc).astype(o_ref.dtype)


def _epi_rope_second_tile(acc, extra, o_ref, *, half, qscale):
    c, s1, s2 = (e[...] for e in extra)
    acc = acc * qscale
    for t in range(acc.shape[1] // LANES):
        sl = slice(t * LANES, (t + 1) * LANES)
        tile = _rope_tile(acc[:, sl], c, s1, s2, half) if t % 2 else acc[:, sl]
        o_ref[:, sl] = tile.astype(o_ref.dtype)


def _epi_rope_tiles(acc, extra, o_ref, *, half, n_rope_blocks, n_q_blocks, qscale):
    j = pl.program_id(1)

    @pl.when(j < n_rope_blocks)
    def _():
        c, s1, s2 = (e[...] for e in extra)
        scaled = acc * jnp.where(j < n_q_blocks, qscale, 1.0).astype(F32)
        for t in range(acc.shape[1] // LANES):
            sl = slice(t * LANES, (t + 1) * LANES)
            o_ref[:, sl] = _rope_tile(scaled[:, sl], c, s1, s2, half).astype(o_ref.dtype)

    @pl.when(j >= n_rope_blocks)
    def _():
        o_ref[...] = acc.astype(o_ref.dtype)


def _epi_add_second_tile(acc, extra, o_ref, *, n_add_blocks):
    j = pl.program_id(1)

    @pl.when(j < n_add_blocks)
    def _():
        shared = extra[0][...]
        for t in range(acc.shape[1] // LANES):
            sl = slice(t * LANES, (t + 1) * LANES)
            tile = acc[:, sl] + shared if t % 2 else acc[:, sl]
            o_ref[:, sl] = tile.astype(o_ref.dtype)

    @pl.when(j >= n_add_blocks)
    def _():
        o_ref[...] = acc.astype(o_ref.dtype)


X_TILE_BYTES = 8 * 1024 * 1024


def _mm_tiles(M, K, N, tm, tn):
    if tm is None:
        tm = 1024 if 1024 * K * 2 <= X_TILE_BYTES else 512
    if tn is None:
        tn = 512
    return _blk(M, tm, 8), _blk(N, tn, LANES)


def _matmul(x, w, *, out_dtype, epilogue=None, extra=(), extra_specs=(), tm=None, tn=None,
            x_col_block=0, name="matmul"):
    M = x.shape[0]
    K, N = w.shape
    tm, tn = _mm_tiles(M, K, N, tm, tn)
    if epilogue is None:
        epilogue = functools.partial(_epi_plain, act=None)
    return pl.pallas_call(
        functools.partial(_mm_body, n_extra=len(extra), epilogue=epilogue),
        out_shape=jax.ShapeDtypeStruct((M, N), out_dtype),
        grid=(M // tm, N // tn),
        in_specs=[pl.BlockSpec((tm, K), lambda i, j: (i, x_col_block)),
                  pl.BlockSpec((K, tn), lambda i, j: (0, j))] + list(extra_specs),
        out_specs=pl.BlockSpec((tm, tn), lambda i, j: (i, j)),
        compiler_params=_params("parallel", "arbitrary"),
        name=name,
    )(x, w, *extra)


def _matmul_residual(x, w, res, *, tm=None, tn=None, name="matmul_residual"):
    M = x.shape[0]
    K, N = w.shape
    tm_, tn_ = _mm_tiles(M, K, N, tm, tn)
    return _matmul(x, w, out_dtype=F32, epilogue=_epi_residual, extra=(res,),
                   extra_specs=(pl.BlockSpec((tm_, tn_), lambda i, j: (i, j)),), tm=tm, tn=tn, name=name)


def _table_specs(T, tm):
    nt = T // tm
    return tuple(pl.BlockSpec((tm, LANES), lambda i, j: (i % nt, 0)) for _ in range(3))


def _swiglu_body(x_ref, wg_ref, wu_ref, o_ref):
    x = x_ref[...]
    g = jnp.dot(x, wg_ref[...], preferred_element_type=F32)
    u = jnp.dot(x, wu_ref[...], preferred_element_type=F32)
    o_ref[...] = (g * jax.nn.sigmoid(g) * u).astype(o_ref.dtype)


def _swiglu_up(x, w_gu, hidden, *, tm=1024, tn=256):
    M, K = x.shape
    tm = _blk(M, tm, 8)
    tn = _blk(hidden, tn, LANES)
    nb = hidden // tn
    return pl.pallas_call(
        _swiglu_body,
        out_shape=jax.ShapeDtypeStruct((M, hidden), BF16),
        grid=(M // tm, nb),
        in_specs=[pl.BlockSpec((tm, K), lambda i, j: (i, 0)),
                  pl.BlockSpec((K, tn), lambda i, j: (0, j)),
                  pl.BlockSpec((K, tn), lambda i, j: (0, j + nb))],
        out_specs=pl.BlockSpec((tm, tn), lambda i, j: (i, j)),
        compiler_params=_params("parallel", "arbitrary"),
        name="swiglu_up",
    )(x, w_gu, w_gu)


def _flash_body(*refs, n_maps, dqk, dv, diff, lambda_init, bks):
    if diff:
        q_ref, k_ref, vt_ref, lam_ref, g_ref, o_ref, m_scr, acc_scr = refs
    else:
        q_ref, k_ref, vt_ref, o_ref, m_scr, acc_scr = refs
    j = pl.program_id(3)

    @pl.when(j == 0)
    def _():
        m_scr[...] = jnp.full(m_scr.shape, -jnp.inf, F32)
        acc_scr[...] = jnp.zeros(acc_scr.shape, F32)

    q = q_ref[0]
    maps = [slice(mp * dqk, (mp + 1) * dqk) for mp in range(n_maps)]
    m = [m_scr[mp] for mp in range(n_maps)]
    acc = [acc_scr[mp] for mp in range(n_maps)]
    n_sub = k_ref.shape[1] // bks
    st = []
    for sub in range(n_sub):
        k = k_ref[0, sub * bks:(sub + 1) * bks, :]
        st.append([lax.dot_general(k[:, sl], q[:, sl], NT, preferred_element_type=F32) for sl in maps])
    for sub in range(n_sub):
        vt = vt_ref[0, :, sub * bks:(sub + 1) * bks]
        for mp in range(n_maps):
            s_t = st[sub][mp]
            m_new = jnp.maximum(m[mp], jnp.max(s_t, axis=0, keepdims=True))
            alpha = jnp.exp2(m[mp] - m_new)
            p = jnp.exp2(s_t - m_new).astype(vt.dtype)
            acc[mp] = alpha * acc[mp] + jnp.dot(vt, p, preferred_element_type=F32)
            m[mp] = m_new
    for mp in range(n_maps):
        m_scr[mp] = m[mp]
        acc_scr[mp] = acc[mp]

    @pl.when(j == pl.num_programs(3) - 1)
    def _():
        o = [(a[:dv] / a[dv:dv + 1]).T for a in acc]
        if diff:
            lp = lam_ref[...]
            lam = (jnp.exp(jnp.sum(lp[0:1] * lp[1:2], axis=-1, keepdims=True))
                   - jnp.exp(jnp.sum(lp[2:3] * lp[3:4], axis=-1, keepdims=True)) + lambda_init)
            od = o[0] - lam * o[1]
            y = od * lax.rsqrt(jnp.mean(od * od, axis=-1, keepdims=True) + DIFF_EPS) * g_ref[...]
            o_ref[0] = (y * (1.0 - lambda_init)).astype(o_ref.dtype)
        else:
            o_ref[0] = o[0].astype(o_ref.dtype)


ONES_ROWS = 16


def _v_transposed(v, heads, dv):
    B, T = v.shape[0], v.shape[1]
    vt = jnp.swapaxes(v, 1, 2).reshape(B, heads, dv, T)
    ones = jnp.ones((B, heads, ONES_ROWS, T), v.dtype)
    return jnp.concatenate([vt, ones], axis=2).reshape(B, heads * (dv + ONES_ROWS), T)


def _flash(q_arr, k_arr, v_arr, *, heads, n_maps, dqk, dv, q_blk0, k_blk0,
           diff_params=None, lambda_init=0.0, bq=512, bk=4096, bks=256):
    B, T = q_arr.shape[0], q_arr.shape[1]
    bq = _blk(T, bq, LANES)
    bk = _blk(T, bk, LANES)
    bks = _blk(bk, bks, LANES)
    wqk = n_maps * dqk
    dve = dv + ONES_ROWS
    diff = diff_params is not None
    in_specs = [pl.BlockSpec((1, bq, wqk), lambda b, h, i, j: (b, i, q_blk0 + h)),
                pl.BlockSpec((1, bk, wqk), lambda b, h, i, j: (b, j, k_blk0 + h)),
                pl.BlockSpec((1, dve, bk), lambda b, h, i, j: (b, h, j))]
    args = [q_arr, k_arr, _v_transposed(v_arr, heads, dv)]
    if diff:
        lam, subln = diff_params
        in_specs += [pl.BlockSpec(lam.shape, lambda b, h, i, j: (0, 0)),
                     pl.BlockSpec(subln.shape, lambda b, h, i, j: (0, 0))]
        args += [lam, subln]
    return pl.pallas_call(
        functools.partial(_flash_body, n_maps=n_maps, dqk=dqk, dv=dv, diff=diff,
                          lambda_init=lambda_init, bks=bks),
        out_shape=jax.ShapeDtypeStruct((B, T, heads * dv), BF16),
        grid=(B, heads, T // bq, T // bk),
        in_specs=in_specs,
        out_specs=pl.BlockSpec((1, bq, dv), lambda b, h, i, j: (b, i, h)),
        scratch_shapes=[pltpu.VMEM((n_maps, 1, bq), F32),
                        pltpu.VMEM((n_maps, dve, bq), F32)],
        compiler_params=_params("parallel", "parallel", "parallel", "arbitrary"),
        name="flash_diff" if diff else "flash_mla",
    )(*args)


def _split_dot(m_exact, x, n_parts):
    acc = None
    rem = x
    for _ in range(n_parts):
        part = rem.astype(BF16)
        rem = rem - part.astype(F32)
        d = lax.dot_general(m_exact, part, NN, preferred_element_type=F32)
        acc = d if acc is None else acc + d
    return acc


def _wkv_body(r_ref, k_ref, v_ref, zw_ref, za_ref, w0_ref, a0_ref, kk_ref, ka_ref, o_ref, s_scr,
              *, reverse, L, pairs):
    c = pl.program_id(2)

    @pl.when(c == 0)
    def _():
        s_scr[...] = jnp.zeros(s_scr.shape, F32)

    L2 = 2 * L
    ri = lax.broadcasted_iota(jnp.int32, (L, L), 0)
    ci = lax.broadcasted_iota(jnp.int32, (L, L), 1)
    tri = ((ri <= ci) if reverse else (ri >= ci)).astype(BF16)
    r2 = lax.broadcasted_iota(jnp.int32, (L2, L2), 0)
    c2 = lax.broadcasted_iota(jnp.int32, (L2, L2), 1)
    strict = (r2 < c2) if reverse else (r2 > c2)
    incl = (r2 <= c2) if reverse else (r2 >= c2)
    eye = jnp.where(r2 == c2, 1.0, 0.0).astype(F32)
    lane = lax.broadcasted_iota(jnp.int32, (L, LANES), 1)
    head0 = lane < RWKV_HEAD_DIM
    hr = lax.broadcasted_iota(jnp.int32, (LANES, LANES), 0) // RWKV_HEAD_DIM
    hc = lax.broadcasted_iota(jnp.int32, (LANES, LANES), 1) // RWKV_HEAD_DIM
    head_ones = jnp.where(hr == hc, 1.0, 0.0).astype(BF16)

    def stack(q):
        return jnp.concatenate([jnp.where(head0, q, 0.0), jnp.where(head0, 0.0, q)], axis=0).astype(BF16)

    r_all = r_ref[0]
    k_all = k_ref[0]
    lw_all = -math.exp(-0.5) * jax.nn.sigmoid(w0_ref[...] + zw_ref[0])
    asig = jax.nn.sigmoid(a0_ref[...] + za_ref[0])
    kkr = k_all * kk_ref[...]
    kkr2 = kkr * kkr
    n2 = jnp.concatenate([_head_sum(kkr2[:, p * LANES:(p + 1) * LANES], head_ones) for p in range(pairs)],
                         axis=1)
    kk = kkr / jnp.maximum(jnp.sqrt(n2), 1e-12)
    kd = k_all * (1.0 + (asig - 1.0) * ka_ref[...])
    b_all = kk * asig
    cum = _split_dot(tri, lw_all, 3)
    tot_all = jnp.sum(lw_all, axis=0, keepdims=True)
    at_all = -kk * jnp.exp(cum - lw_all)
    rt_all = r_all * jnp.exp(cum)
    pinv = jnp.exp(-cum)
    pend = jnp.exp(tot_all - cum)
    bi_all, ki_all = b_all * pinv, kd * pinv
    be_all, ke_all = b_all * pend, kd * pend
    decay_all = jnp.exp(tot_all)

    P = range(pairs)
    lanes = [slice(p * LANES, (p + 1) * LANES) for p in P]
    dotf = lambda a, b, dims: lax.dot_general(a, b, dims, preferred_element_type=F32)
    As = [stack(at_all[:, sl]) for sl in lanes]
    Bs = [stack(bi_all[:, sl]) for sl in lanes]
    m_ab = [jnp.where(strict, dotf(As[p], Bs[p], NT), 0.0) for p in P]

    inv = None
    s = 1
    while s < L:
        sh = s.bit_length() - 1
        off_diag = ((r2 >> (sh + 1)) == (c2 >> (sh + 1))) & ((r2 >> sh) != (c2 >> sh))
        e = [jnp.where(off_diag, m_ab[p], 0.0) for p in P]
        if s == 1:
            inv = [eye + e[p] for p in P]
        else:
            invb = [inv[p].astype(BF16) for p in P]
            t = [dotf(e[p].astype(BF16), invb[p], NN).astype(BF16) for p in P]
            inv = [inv[p] + dotf(invb[p], t[p], NN) for p in P]
        s *= 2

    Rs = [stack(rt_all[:, sl]) for sl in lanes]
    Ks = [stack(ki_all[:, sl]) for sl in lanes]
    Vs = [stack(v_ref[0, :, sl]) for sl in lanes]
    m_ak = [jnp.where(strict, dotf(As[p], Ks[p], NT), 0.0).astype(BF16) for p in P]
    m_rb = [jnp.where(incl, dotf(Rs[p], Bs[p], NT), 0.0).astype(BF16) for p in P]
    m_rk = [jnp.where(incl, dotf(Rs[p], Ks[p], NT), 0.0).astype(BF16) for p in P]
    state = [s_scr[p] for p in P]
    sb = [state[p].astype(BF16) for p in P]
    w_s = [dotf(As[p], sb[p], NT) + dotf(m_ak[p], Vs[p], NN) for p in P]
    u_s = [dotf(inv[p].astype(BF16), w_s[p].astype(BF16), NN).astype(BF16) for p in P]
    o_s = [dotf(Rs[p], sb[p], NT) + dotf(m_rb[p], u_s[p], NN) + dotf(m_rk[p], Vs[p], NN) for p in P]
    Bh = [stack(be_all[:, sl]) for sl in lanes]
    Kh = [stack(ke_all[:, sl]) for sl in lanes]
    new_state = [state[p] * decay_all[:, lanes[p]] + dotf(u_s[p], Bh[p], TN) + dotf(Vs[p], Kh[p], TN)
                 for p in P]
    o_ref[0] = jnp.concatenate([o[:L] + o[L:] for o in o_s], axis=1)
    s_scr[...] = jnp.stack(new_state)


def _head_sum(x, head_ones):
    hi = x.astype(BF16)
    lo = (x - hi.astype(F32)).astype(BF16)
    return (lax.dot_general(hi, head_ones, NN, preferred_element_type=F32)
            + lax.dot_general(lo, head_ones, NN, preferred_element_type=F32))


def _wkv(r, k, v, zw, za, w0, a0, k_k, k_a, *, reverse, pairs=16):
    B, T, C = r.shape
    L = WKV_CHUNK
    pairs = min(pairs, C // LANES)
    wl = pairs * LANES
    nc = T // L

    def tmap(b, g, c):
        return (b, (nc - 1 - c) if reverse else c, g)

    seq = pl.BlockSpec((1, L, wl), tmap)
    par = pl.BlockSpec((1, wl), lambda b, g, c: (0, g))
    return pl.pallas_call(
        functools.partial(_wkv_body, reverse=reverse, L=L, pairs=pairs),
        out_shape=jax.ShapeDtypeStruct((B, T, C), F32),
        grid=(B, C // wl, nc),
        in_specs=[seq, seq, seq, seq, seq, par, par, par, par],
        out_specs=seq,
        scratch_shapes=[pltpu.VMEM((pairs, LANES, LANES), F32)],
        compiler_params=_params("parallel", "parallel", "arbitrary"),
        name="wkv7_rev" if reverse else "wkv7_fwd",
    )(r, k, v, zw, za, w0, a0, k_k, k_a)


def _wkv_out_body(of_ref, ob_ref, r_ref, k_ref, v_ref, za0_ref, za1_ref, g_ref,
                  a00_ref, a01_ref, ka_ref, rk_ref, lw_ref, lb_ref, o_ref):
    hr = lax.broadcasted_iota(jnp.int32, (LANES, LANES), 0) // RWKV_HEAD_DIM
    hc = lax.broadcasted_iota(jnp.int32, (LANES, LANES), 1) // RWKV_HEAD_DIM
    head_ones = jnp.where(hr == hc, 1.0, 0.0).astype(BF16)
    inv_n = 1.0 / RWKV_HEAD_DIM
    for t in range(o_ref.shape[1] // LANES):
        sl = slice(t * LANES, (t + 1) * LANES)
        o = of_ref[:, sl] + ob_ref[:, sl]
        mu = _head_sum(o, head_ones) * inv_n
        d = o - mu
        var = _head_sum(d * d, head_ones) * inv_n
        on = d * lax.rsqrt(var + RWKV_GN_EPS) * lw_ref[:, sl] + lb_ref[:, sl]
        asum = jax.nn.sigmoid(a00_ref[:, sl] + za0_ref[:, sl]) + jax.nn.sigmoid(a01_ref[:, sl] + za1_ref[:, sl])
        ksum = k_ref[:, sl] * (2.0 + (asum - 2.0) * ka_ref[:, sl])
        bonus = _head_sum(r_ref[:, sl] * ksum * rk_ref[:, sl], head_ones) * v_ref[:, sl]
        o_ref[:, sl] = ((on + bonus) * g_ref[:, sl]).astype(o_ref.dtype)


def _wkv_out(o_f, o_b, r, k, v, za0, za1, g, a0, k_a, r_k, lnx_w, lnx_b):
    M, C = r.shape
    tm = _blk(M, 256, 8)
    tc = _blk(C, 512, LANES)
    seq = pl.BlockSpec((tm, tc), lambda i, j: (i, j))
    par = pl.BlockSpec((1, tc), lambda i, j: (0, j))
    row = lambda x: x.reshape(1, C).astype(F32)
    return pl.pallas_call(
        _wkv_out_body,
        out_shape=jax.ShapeDtypeStruct((M, C), BF16),
        grid=(M // tm, C // tc),
        in_specs=[seq] * 8 + [par] * 6,
        out_specs=seq,
        compiler_params=_params("parallel", "parallel"),
        name="wkv_out",
    )(o_f, o_b, r, k, v, za0, za1, g, row(a0[0]), row(a0[1]), row(k_a), row(r_k), row(lnx_w), row(lnx_b))


def _ffn(x2, p):
    h = _rmsnorm(x2, p["norm_ffn"], out_dtype=BF16)
    hidden = p["w_down"].shape[0]
    mid = _swiglu_up(h, p["w_gu"], hidden)
    return _matmul_residual(mid, p["w_down"], x2, tm=512, tn=256, name="ffn_down")


def _rwkv_layer(x, p, v_first):
    B, T, C = x.shape
    M = B * T
    x2 = x.reshape(M, C)
    h = _rmsnorm(x2, p["norm_mix"]).reshape(B, T, C)
    hp = jnp.pad(h, ((0, 0), (1, 1), (0, 0)))
    dx = 0.5 * (hp[:, :-2] + hp[:, 2:]) - h
    mix = p["mix"]
    xr, xw, xk, xv, xa, xg = ((h + dx * mix[i]).astype(BF16).reshape(M, C) for i in range(6))
    r = _matmul(xr, p["w_r"], out_dtype=F32, name="rwkv_r")
    k = _matmul(xk, p["w_k"], out_dtype=F32, name="rwkv_k")
    v = _matmul(xv, p["w_v"], out_dtype=F32, name="rwkv_v")
    tw = _matmul(xw, p["decay1"], out_dtype=BF16, epilogue=functools.partial(_epi_plain, act="tanh"),
                 name="rwkv_decay1")
    ta = _matmul(xa, p["a1"], out_dtype=BF16, name="rwkv_a1")
    zw = [_matmul(tw, p["decay2"][d], out_dtype=F32, x_col_block=d, name="rwkv_decay2") for d in range(2)]
    za = [_matmul(ta, p["a2"][d], out_dtype=F32, x_col_block=d, name="rwkv_a2") for d in range(2)]
    if v_first is not None:
        tv = _matmul(xv, p["v1"], out_dtype=BF16, name="rwkv_v1")
        vl = _matmul(tv, p["v2"], out_dtype=F32, name="rwkv_v2")
        v = v + (v_first - v) * jax.nn.sigmoid(p["v0"] + vl)
    tg = _matmul(xg, p["g1"], out_dtype=BF16, epilogue=functools.partial(_epi_plain, act="sigmoid"),
                 name="rwkv_g1")
    g = _matmul(tg, p["g2"], out_dtype=F32, name="rwkv_g2")
    seq = lambda t: t.reshape(B, T, C)
    row = lambda t: t.reshape(1, C).astype(F32)
    o_dir = [
        _wkv(seq(r), seq(k), seq(v), seq(zw[d]), seq(za[d]), row(p["w_decay0"][d]), row(p["a0"][d]),
             row(p["k_k"]), row(p["k_a"]), reverse=bool(d))
        for d in range(2)
    ]
    y = _wkv_out(o_dir[0].reshape(M, C), o_dir[1].reshape(M, C), r, k, v, za[0], za[1], g,
                 p["a0"], p["k_a"], p["r_k"], p["lnx_w"], p["lnx_b"])
    x2 = _matmul_residual(y, p["w_o"], x2, name="rwkv_o")
    return x2, v


def _mla_layer(x, p):
    B, T, C = x.shape
    M = B * T
    x2 = x.reshape(M, C)
    H = p["w_uq"].shape[1] // MLA_QK_PAD
    q_rank = p["q_norm"].shape[0]
    kv_rank = p["kv_norm"].shape[0]
    h = _rmsnorm(x2, p["norm_mix"], out_dtype=BF16)
    lat = _matmul(h, p["w_in"], out_dtype=F32, tn=256, name="mla_in")
    c_q = _rmsnorm(lat, p["q_norm"], out_dtype=BF16, width=q_rank, col_block=0)
    c_kv = _rmsnorm(lat, p["kv_norm"], out_dtype=BF16, width=kv_rank, col_block=q_rank // kv_rank)
    tables = _rope_tables(T, MLA_ROPE_DIM)
    half = MLA_ROPE_DIM // 2
    k_rope = _rope_cols(lat, tables, T, (q_rank + kv_rank) // LANES, half)
    tm = _blk(T, 1024, 8)
    qscale = (MLA_NOPE_DIM + MLA_ROPE_DIM) ** -0.5 * math.log2(math.e)
    q = _matmul(c_q, p["w_uq"], out_dtype=BF16, tm=tm, tn=_blk(H * MLA_QK_PAD, 1024, MLA_QK_PAD),
                epilogue=functools.partial(_epi_rope_second_tile, half=half, qscale=qscale),
                extra=tables, extra_specs=_table_specs(T, _blk(M, tm, 8)), name="mla_uq")
    tm_kv = _blk(M, 1024, 8)
    tn_kv = _blk(H * MLA_V_DIM, 2048, MLA_QK_PAD)
    kv = _matmul(c_kv, p["w_ukv"], out_dtype=BF16, tm=tm_kv, tn=tn_kv,
                 epilogue=functools.partial(_epi_add_second_tile, n_add_blocks=H * MLA_QK_PAD // tn_kv),
                 extra=(k_rope,), extra_specs=(pl.BlockSpec((tm_kv, LANES), lambda i, j: (i, 0)),),
                 name="mla_ukv")
    q3 = q.reshape(B, T, H * MLA_QK_PAD)
    kv3 = kv.reshape(B, T, kv.shape[1])
    o = _flash(q3, kv3, kv3[:, :, H * MLA_QK_PAD:], heads=H, n_maps=1, dqk=MLA_QK_PAD, dv=MLA_V_DIM,
               q_blk0=0, k_blk0=0)
    return _matmul_residual(o.reshape(M, H * MLA_V_DIM), p["w_o"], x2, name="mla_o")


def _diff_layer(x, p, lambda_init):
    B, T, C = x.shape
    M = B * T
    x2 = x.reshape(M, C)
    H = C // (2 * DIFF_HEAD_DIM)
    h = _rmsnorm(x2, p["norm_mix"], out_dtype=BF16)
    tables = _rope_tables(T, DIFF_ROT_DIM)
    tm = _blk(T, 1024, 8)
    tn = _blk(C, 512, LANES)
    qkv = _matmul(h, p["w_qkv"], out_dtype=BF16, tm=tm, tn=tn,
                  epilogue=functools.partial(_epi_rope_tiles, half=DIFF_ROT_DIM // 2,
                                             n_rope_blocks=2 * C // tn, n_q_blocks=C // tn,
                                             qscale=DIFF_HEAD_DIM ** -0.5 * math.log2(math.e)),
                  extra=tables, extra_specs=_table_specs(T, _blk(M, tm, 8)), name="diff_qkv")
    qkv3 = qkv.reshape(B, T, 3 * C)
    o = _flash(qkv3, qkv3, qkv3[:, :, 2 * C:], heads=H, n_maps=2, dqk=DIFF_HEAD_DIM,
               dv=2 * DIFF_HEAD_DIM, q_blk0=0, k_blk0=H,
               diff_params=(p["lambda"].astype(F32), p["subln"].reshape(1, -1).astype(F32)),
               lambda_init=lambda_init)
    return _matmul_residual(o.reshape(M, C), p["w_o"], x2, name="diff_o")


def _trunk(x, layers, final_norm):
    B, T, C = x.shape
    v_first = None
    for i, p in enumerate(layers):
        kind = p["kind"]
        if kind == "rwkv":
            x2, v_layer = _rwkv_layer(x, p, v_first)
            if v_first is None:
                v_first = v_layer
        elif kind == "mla":
            x2 = _mla_layer(x, p)
        else:
            x2 = _diff_layer(x, p, 0.8 - 0.6 * math.exp(-0.3 * i))
        x2 = _ffn(x2, p)
        x = x2.reshape(B, T, C)
    return _rmsnorm(x.reshape(B * T, C), final_norm).reshape(B, T, C)


def _prep_ffn(p):
    return {"norm_ffn": p["norm_ffn"], "w_gu": p["w_gu"].astype(BF16), "w_down": p["w_down"].astype(BF16)}


def _prep_rwkv(p):
    b = lambda w: w.astype(BF16)
    out = dict(kind="rwkv", norm_mix=p["norm_mix"], mix=p["mix"],
               w_r=b(p["w_rkv"][0]), w_k=b(p["w_rkv"][1]), w_v=b(p["w_rkv"][2]),
               w_decay0=p["w_decay0"], a0=p["a0"], k_k=p["k_k"], k_a=p["k_a"], r_k=p["r_k"],
               lnx_w=p["lnx_w"], lnx_b=p["lnx_b"], w_o=b(p["w_o"]))
    for name1, name2, key1, key2 in (("w_decay1", "w_decay2", "decay1", "decay2"), ("a1", "a2", "a1", "a2")):
        w1 = [_pad_to(p[name1][d], 1, LANES) for d in range(2)]
        out[key1] = b(jnp.concatenate(w1, axis=1))
        out[key2] = [b(_pad_to(p[name2][d], 0, LANES)) for d in range(2)]
    if "v0" in p:
        out["v0"] = p["v0"]
        out["v1"] = b(_pad_to(p["v1"], 1, LANES))
        out["v2"] = b(_pad_to(p["v2"], 0, LANES))
    out["g1"] = b(_pad_to(p["g1"], 1, LANES))
    out["g2"] = b(_pad_to(p["g2"], 0, LANES))
    out.update(_prep_ffn(p))
    return out


def _prep_mla(p):
    b = lambda w: w.astype(BF16)
    q_rank = p["q_norm"].shape[0]
    kv_rank = p["kv_norm"].shape[0]
    qk = MLA_NOPE_DIM + MLA_ROPE_DIM
    H = p["w_uq"].shape[1] // qk
    w_uq = _pad_to(p["w_uq"].reshape(q_rank, H, qk), 2, MLA_QK_PAD).reshape(q_rank, H * MLA_QK_PAD)
    w_ukv = p["w_ukv"].reshape(kv_rank, H, MLA_NOPE_DIM + MLA_V_DIM)
    w_k = _pad_to(w_ukv[:, :, :MLA_NOPE_DIM], 2, MLA_QK_PAD).reshape(kv_rank, H * MLA_QK_PAD)
    w_v = w_ukv[:, :, MLA_NOPE_DIM:].reshape(kv_rank, H * MLA_V_DIM)
    out = dict(kind="mla", norm_mix=p["norm_mix"], w_in=b(_pad_to(p["w_in"], 1, 256)),
               q_norm=p["q_norm"], kv_norm=p["kv_norm"], w_uq=b(w_uq),
               w_ukv=b(jnp.concatenate([w_k, w_v], axis=1)), w_o=b(p["w_o"]))
    out.update(_prep_ffn(p))
    return out


def _prep_diff(p):
    out = dict(kind="diff", norm_mix=p["norm_mix"], w_qkv=p["w_qkv"].astype(BF16), subln=p["subln"],
               w_o=p["w_o"].astype(BF16))
    out["lambda"] = p["lambda"]
    out.update(_prep_ffn(p))
    return out


def _forward(x_prompt, x_sample, raw_layers, final_norm):
    prep = {"rwkv": _prep_rwkv, "mla": _prep_mla, "diff": _prep_diff}
    kinds = ("rwkv", "mla", "diff")
    layers = [prep[kinds[i % 3]](p) for i, p in enumerate(raw_layers)]
    return (_trunk(x_prompt, layers, final_norm), _trunk(x_sample, layers, final_norm))


def kernel(x_prompt, x_sample, l0_norm_mix, l0_mix, l0_w_rkv, l0_w_decay0, l0_w_decay1, l0_w_decay2, l0_a0, l0_a1, l0_a2, l0_g1, l0_g2, l0_k_k, l0_k_a, l0_r_k, l0_lnx_w, l0_lnx_b, l0_w_o, l0_norm_ffn, l0_w_gu, l0_w_down, l1_norm_mix, l1_w_in, l1_q_norm, l1_kv_norm, l1_w_uq, l1_w_ukv, l1_w_o, l1_norm_ffn, l1_w_gu, l1_w_down, l2_norm_mix, l2_w_qkv, l2_lambda, l2_subln, l2_w_o, l2_norm_ffn, l2_w_gu, l2_w_down, l3_norm_mix, l3_mix, l3_w_rkv, l3_w_decay0, l3_w_decay1, l3_w_decay2, l3_a0, l3_a1, l3_a2, l3_v0, l3_v1, l3_v2, l3_g1, l3_g2, l3_k_k, l3_k_a, l3_r_k, l3_lnx_w, l3_lnx_b, l3_w_o, l3_norm_ffn, l3_w_gu, l3_w_down, final_norm):
    raw_layers = (
        dict(norm_mix=l0_norm_mix, mix=l0_mix, w_rkv=l0_w_rkv, w_decay0=l0_w_decay0, w_decay1=l0_w_decay1,
             w_decay2=l0_w_decay2, a0=l0_a0, a1=l0_a1, a2=l0_a2, g1=l0_g1, g2=l0_g2, k_k=l0_k_k, k_a=l0_k_a,
             r_k=l0_r_k, lnx_w=l0_lnx_w, lnx_b=l0_lnx_b, w_o=l0_w_o,
             norm_ffn=l0_norm_ffn, w_gu=l0_w_gu, w_down=l0_w_down),
        dict(norm_mix=l1_norm_mix, w_in=l1_w_in, q_norm=l1_q_norm, kv_norm=l1_kv_norm, w_uq=l1_w_uq,
             w_ukv=l1_w_ukv, w_o=l1_w_o, norm_ffn=l1_norm_ffn, w_gu=l1_w_gu, w_down=l1_w_down),
        dict(norm_mix=l2_norm_mix, w_qkv=l2_w_qkv, subln=l2_subln, w_o=l2_w_o,
             norm_ffn=l2_norm_ffn, w_gu=l2_w_gu, w_down=l2_w_down, **{"lambda": l2_lambda}),
        dict(norm_mix=l3_norm_mix, mix=l3_mix, w_rkv=l3_w_rkv, w_decay0=l3_w_decay0, w_decay1=l3_w_decay1,
             w_decay2=l3_w_decay2, a0=l3_a0, a1=l3_a1, a2=l3_a2, v0=l3_v0, v1=l3_v1, v2=l3_v2,
             g1=l3_g1, g2=l3_g2, k_k=l3_k_k, k_a=l3_k_a, r_k=l3_r_k, lnx_w=l3_lnx_w, lnx_b=l3_lnx_b, w_o=l3_w_o,
             norm_ffn=l3_norm_ffn, w_gu=l3_w_gu, w_down=l3_w_down),
    )
    return _forward(x_prompt, x_sample, raw_layers, final_norm)
```

```python
import functools
import math

import jax
import jax.numpy as jnp
from jax import lax
from jax.experimental import pallas as pl
from jax.experimental.pallas import tpu as pltpu

F32 = jnp.float32
BF16 = jnp.bfloat16

LANES = 128
VMEM_LIMIT = 56 * 1024 * 1024

RMS_EPS = 1e-6
ROPE_THETA = 500000.0
RWKV_HEAD_DIM = 64
RWKV_GN_EPS = 64e-5
WKV_CHUNK = 64
MLA_NOPE_DIM = 128
MLA_ROPE_DIM = 64
MLA_V_DIM = 128
MLA_QK_PAD = 256
DIFF_HEAD_DIM = 128
DIFF_ROT_DIM = DIFF_HEAD_DIM // 4
DIFF_EPS = 1e-5
DEPTH = 4

NN = (((1,), (0,)), ((), ()))
NT = (((1,), (1,)), ((), ()))
TN = (((0,), (0,)), ((), ()))


def _blk(n, pref, align):
    if n <= pref:
        return n
    d = (pref // align) * align
    while d >= align:
        if n % d == 0:
            return d
        d -= align
    raise ValueError(f"no block for {n} (pref {pref}, align {align})")


def _params(*sem):
    return pltpu.CompilerParams(dimension_semantics=sem, vmem_limit_bytes=VMEM_LIMIT)


def _pad_to(x, axis, mult):
    n = x.shape[axis]
    pad = (-n) % mult
    if pad == 0:
        return x
    widths = [(0, 0)] * x.ndim
    widths[axis] = (0, pad)
    return jnp.pad(x, widths)


def _rmsnorm_body(x_ref, g_ref, o_ref, *, eps):
    x = x_ref[...].astype(F32)
    y = x * lax.rsqrt(jnp.mean(x * x, axis=-1, keepdims=True) + eps)
    o_ref[...] = (y * g_ref[...]).astype(o_ref.dtype)


def _rmsnorm(x, g, *, eps=RMS_EPS, out_dtype=F32, width=None, col_block=0):
    M = x.shape[0]
    width = x.shape[1] if width is None else width
    tm = _blk(M, 512, 8)
    return pl.pallas_call(
        functools.partial(_rmsnorm_body, eps=eps),
        out_shape=jax.ShapeDtypeStruct((M, width), out_dtype),
        grid=(M // tm,),
        in_specs=[pl.BlockSpec((tm, width), lambda i: (i, col_block)),
                  pl.BlockSpec((1, width), lambda i: (0, 0))],
        out_specs=pl.BlockSpec((tm, width), lambda i: (i, 0)),
        compiler_params=_params("parallel"),
        name="rmsnorm",
    )(x, g.reshape(1, width).astype(F32))


def _rope_tables(T, dim):
    half = dim // 2
    inv_freq = 1.0 / (ROPE_THETA ** (jnp.arange(0, dim, 2, dtype=F32) / dim))
    ang = jnp.arange(T, dtype=F32)[:, None] * inv_freq[None, :]
    c, s = jnp.cos(ang), jnp.sin(ang)
    ones = jnp.ones((T, LANES - dim), F32)
    zeros = jnp.zeros((T, LANES - dim), F32)
    zh = jnp.zeros((T, half), F32)
    C = jnp.concatenate([c, c, ones], axis=1)
    S1 = jnp.concatenate([-s, zh, zeros], axis=1)
    S2 = jnp.concatenate([zh, s, zeros], axis=1)
    return C, S1, S2


def _rope_tile(t, c, s1, s2, half):
    return t * c + pltpu.roll(t, LANES - half, 1) * s1 + pltpu.roll(t, half, 1) * s2


def _rope_cols_body(x_ref, c_ref, s1_ref, s2_ref, o_ref, *, half):
    o_ref[...] = _rope_tile(x_ref[...].astype(F32), c_ref[...], s1_ref[...], s2_ref[...], half)


def _rope_cols(x, tables, T, col_block, half):
    M = x.shape[0]
    tm = _blk(T, 512, 8)
    nt = T // tm
    tab = pl.BlockSpec((tm, LANES), lambda i: (i % nt, 0))
    return pl.pallas_call(
        functools.partial(_rope_cols_body, half=half),
        out_shape=jax.ShapeDtypeStruct((M, LANES), F32),
        grid=(M // tm,),
        in_specs=[pl.BlockSpec((tm, LANES), lambda i: (i, col_block)), tab, tab, tab],
        out_specs=pl.BlockSpec((tm, LANES), lambda i: (i, 0)),
        compiler_params=_params("parallel"),
        name="rope_cols",
    )(x, *tables)


def _mm_body(*refs, n_extra, epilogue):
    x_ref, w_ref = refs[0], refs[1]
    extra = refs[2:2 + n_extra]
    o_ref = refs[2 + n_extra]
    acc = jnp.dot(x_ref[...], w_ref[...], preferred_element_type=F32)
    epilogue(acc, extra, o_ref)


def _epi_plain(acc, extra, o_ref, *, act):
    if act == "tanh":
        acc = jnp.tanh(acc)
    elif act == "sigmoid":
        acc = jax.nn.sigmoid(acc)
    o_ref[...] = acc.astype(o_ref.dtype)


def _epi_residual(acc, extra, o_ref):
    o_ref[...] = (extra[0][...] + acc).astype(o_ref.dtype)


def _epi_value_residual(acc, extra, o_ref):
    v = extra[0][...]
    o_ref[...] = (v + (extra[1][...] - v) * jax.nn.sigmoid(extra[2][...] + acc)).astype(o_ref.dtype)


def _epi_rope_second_tile(acc, extra, o_ref, *, half, qscale):
    c, s1, s2 = (e[...] for e in extra)
    acc = acc * qscale
    for t in range(acc.shape[1] // LANES):
        sl = slice(t * LANES, (t + 1) * LANES)
        tile = _rope_tile(acc[:, sl], c, s1, s2, half) if t % 2 else acc[:, sl]
        o_ref[:, sl] = tile.astype(o_ref.dtype)


def _epi_rope_tiles(acc, extra, o_ref, *, half, n_rope_blocks, n_q_blocks, qscale):
    j = pl.program_id(1)

    @pl.when(j < n_rope_blocks)
    def _():
        c, s1, s2 = (e[...] for e in extra)
        scaled = acc * jnp.where(j < n_q_blocks, qscale, 1.0).astype(F32)
        for t in range(acc.shape[1] // LANES):
            sl = slice(t * LANES, (t + 1) * LANES)
            o_ref[:, sl] = _rope_tile(scaled[:, sl], c, s1, s2, half).astype(o_ref.dtype)

    @pl.when(j >= n_rope_blocks)
    def _():
        o_ref[...] = acc.astype(o_ref.dtype)


def _epi_add_second_tile(acc, extra, o_ref, *, n_add_blocks):
    j = pl.program_id(1)

    @pl.when(j < n_add_blocks)
    def _():
        shared = extra[0][...]
        for t in range(acc.shape[1] // LANES):
            sl = slice(t * LANES, (t + 1) * LANES)
            tile = acc[:, sl] + shared if t % 2 else acc[:, sl]
            o_ref[:, sl] = tile.astype(o_ref.dtype)

    @pl.when(j >= n_add_blocks)
    def _():
        o_ref[...] = acc.astype(o_ref.dtype)


X_TILE_BYTES = 8 * 1024 * 1024


def _mm_tiles(M, K, N, tm, tn):
    if tm is None:
        tm = 1024 if 1024 * K * 2 <= X_TILE_BYTES else 512
    if tn is None:
        tn = 512
    return _blk(M, tm, 8), _blk(N, tn, LANES)


def _matmul(x, w, *, out_dtype, epilogue=None, extra=(), extra_specs=(), tm=None, tn=None,
            name="matmul"):
    M = x.shape[0]
    K, N = w.shape
    tm, tn = _mm_tiles(M, K, N, tm, tn)
    if epilogue is None:
        epilogue = functools.partial(_epi_plain, act=None)
    return pl.pallas_call(
        functools.partial(_mm_body, n_extra=len(extra), epilogue=epilogue),
        out_shape=jax.ShapeDtypeStruct((M, N), out_dtype),
        grid=(M // tm, N // tn),
        in_specs=[pl.BlockSpec((tm, K), lambda i, j: (i, 0)),
                  pl.BlockSpec((K, tn), lambda i, j: (0, j))] + list(extra_specs),
        out_specs=pl.BlockSpec((tm, tn), lambda i, j: (i, j)),
        compiler_params=_params("parallel", "arbitrary"),
        name=name,
    )(x, w, *extra)


def _matmul_residual(x, w, res, *, tm=None, tn=None, name="matmul_residual"):
    M = x.shape[0]
    K, N = w.shape
    tm_, tn_ = _mm_tiles(M, K, N, tm, tn)
    return _matmul(x, w, out_dtype=F32, epilogue=_epi_residual, extra=(res,),
                   extra_specs=(pl.BlockSpec((tm_, tn_), lambda i, j: (i, j)),), tm=tm, tn=tn, name=name)


def _table_specs(T, tm):
    nt = T // tm
    return tuple(pl.BlockSpec((tm, LANES), lambda i, j: (i % nt, 0)) for _ in range(3))


def _swiglu_body(x_ref, wg_ref, wu_ref, o_ref):
    x = x_ref[...]
    g = jnp.dot(x, wg_ref[...], preferred_element_type=F32)
    u = jnp.dot(x, wu_ref[...], preferred_element_type=F32)
    o_ref[...] = (g * jax.nn.sigmoid(g) * u).astype(o_ref.dtype)


def _swiglu_up(x, w_gu, hidden, *, tm=1024, tn=256):
    M, K = x.shape
    tm = _blk(M, tm, 8)
    tn = _blk(hidden, tn, LANES)
    nb = hidden // tn
    return pl.pallas_call(
        _swiglu_body,
        out_shape=jax.ShapeDtypeStruct((M, hidden), BF16),
        grid=(M // tm, nb),
        in_specs=[pl.BlockSpec((tm, K), lambda i, j: (i, 0)),
                  pl.BlockSpec((K, tn), lambda i, j: (0, j)),
                  pl.BlockSpec((K, tn), lambda i, j: (0, j + nb))],
        out_specs=pl.BlockSpec((tm, tn), lambda i, j: (i, j)),
        compiler_params=_params("parallel", "arbitrary"),
        name="swiglu_up",
    )(x, w_gu, w_gu)


def _flash_body(*refs, n_maps, dqk, dv, diff, lambda_init, bks):
    if diff:
        q_ref, k_ref, vt_ref, lam_ref, g_ref, o_ref, m_scr, acc_scr = refs
    else:
        q_ref, k_ref, vt_ref, o_ref, m_scr, acc_scr = refs
    j = pl.program_id(3)

    @pl.when(j == 0)
    def _():
        m_scr[...] = jnp.full(m_scr.shape, -jnp.inf, F32)
        acc_scr[...] = jnp.zeros(acc_scr.shape, F32)

    q = q_ref[0]
    maps = [slice(mp * dqk, (mp + 1) * dqk) for mp in range(n_maps)]
    m = [m_scr[mp] for mp in range(n_maps)]
    acc = [acc_scr[mp] for mp in range(n_maps)]
    n_sub = k_ref.shape[1] // bks
    st = []
    for sub in range(n_sub):
        k = k_ref[0, sub * bks:(sub + 1) * bks, :]
        st.append([lax.dot_general(k[:, sl], q[:, sl], NT, preferred_element_type=F32) for sl in maps])
    for sub in range(n_sub):
        vt = vt_ref[0, :, sub * bks:(sub + 1) * bks]
        for mp in range(n_maps):
            s_t = st[sub][mp]
            m_new = jnp.maximum(m[mp], jnp.max(s_t, axis=0, keepdims=True))
            alpha = jnp.exp2(m[mp] - m_new)
            p = jnp.exp2(s_t - m_new).astype(vt.dtype)
            acc[mp] = alpha * acc[mp] + jnp.dot(vt, p, preferred_element_type=F32)
            m[mp] = m_new
    for mp in range(n_maps):
        m_scr[mp] = m[mp]
        acc_scr[mp] = acc[mp]

    @pl.when(j == pl.num_programs(3) - 1)
    def _():
        o = [(a[:dv] / a[dv:dv + 1]).T for a in acc]
        if diff:
            lp = lam_ref[...]
            lam = (jnp.exp(jnp.sum(lp[0:1] * lp[1:2], axis=-1, keepdims=True))
                   - jnp.exp(jnp.sum(lp[2:3] * lp[3:4], axis=-1, keepdims=True)) + lambda_init)
            od = o[0] - lam * o[1]
            y = od * lax.rsqrt(jnp.mean(od * od, axis=-1, keepdims=True) + DIFF_EPS) * g_ref[...]
            o_ref[0] = (y * (1.0 - lambda_init)).astype(o_ref.dtype)
        else:
            o_ref[0] = o[0].astype(o_ref.dtype)


ONES_ROWS = 16


def _v_transposed(v, heads, dv):
    B, T = v.shape[0], v.shape[1]
    vt = jnp.swapaxes(v, 1, 2).reshape(B, heads, dv, T)
    ones = jnp.ones((B, heads, ONES_ROWS, T), v.dtype)
    return jnp.concatenate([vt, ones], axis=2).reshape(B, heads * (dv + ONES_ROWS), T)


def _flash(q_arr, k_arr, v_arr, *, heads, n_maps, dqk, dv, q_blk0, k_blk0,
           diff_params=None, lambda_init=0.0, bq=512, bk=8192, bks=256):
    B, T = q_arr.shape[0], q_arr.shape[1]
    bq = _blk(T, bq, LANES)
    bk = _blk(T, bk, LANES)
    bks = _blk(bk, bks, LANES)
    wqk = n_maps * dqk
    dve = dv + ONES_ROWS
    diff = diff_params is not None
    in_specs = [pl.BlockSpec((1, bq, wqk), lambda b, h, i, j: (b, i, q_blk0 + h)),
                pl.BlockSpec((1, bk, wqk), lambda b, h, i, j: (b, j, k_blk0 + h)),
                pl.BlockSpec((1, dve, bk), lambda b, h, i, j: (b, h, j))]
    args = [q_arr, k_arr, _v_transposed(v_arr, heads, dv)]
    if diff:
        lam, subln = diff_params
        in_specs += [pl.BlockSpec(lam.shape, lambda b, h, i, j: (0, 0)),
                     pl.BlockSpec(subln.shape, lambda b, h, i, j: (0, 0))]
        args += [lam, subln]
    return pl.pallas_call(
        functools.partial(_flash_body, n_maps=n_maps, dqk=dqk, dv=dv, diff=diff,
                          lambda_init=lambda_init, bks=bks),
        out_shape=jax.ShapeDtypeStruct((B, T, heads * dv), BF16),
        grid=(B, heads, T // bq, T // bk),
        in_specs=in_specs,
        out_specs=pl.BlockSpec((1, bq, dv), lambda b, h, i, j: (b, i, h)),
        scratch_shapes=[pltpu.VMEM((n_maps, 1, bq), F32),
                        pltpu.VMEM((n_maps, dve, bq), F32)],
        compiler_params=_params("parallel", "parallel", "parallel", "arbitrary"),
        name="flash_diff" if diff else "flash_mla",
    )(*args)


def _split_dot(m_exact, x, n_parts):
    acc = None
    rem = x
    for _ in range(n_parts):
        part = rem.astype(BF16)
        rem = rem - part.astype(F32)
        d = lax.dot_general(m_exact, part, NN, preferred_element_type=F32)
        acc = d if acc is None else acc + d
    return acc


def _wkv_body(r_ref, k_ref, v_ref, tw_ref, wd2_ref, ta_ref, wa2_ref, w0_ref, a0_ref, kk_ref, ka_ref,
              o_ref, s_scr,
              *, reverse, L, pairs):
    c = pl.program_id(2)

    @pl.when(c == 0)
    def _():
        s_scr[...] = jnp.zeros(s_scr.shape, F32)

    L2 = 2 * L
    ri = lax.broadcasted_iota(jnp.int32, (L, L), 0)
    ci = lax.broadcasted_iota(jnp.int32, (L, L), 1)
    tri = ((ri <= ci) if reverse else (ri >= ci)).astype(BF16)
    r2 = lax.broadcasted_iota(jnp.int32, (L2, L2), 0)
    c2 = lax.broadcasted_iota(jnp.int32, (L2, L2), 1)
    strict = (r2 < c2) if reverse else (r2 > c2)
    incl = (r2 <= c2) if reverse else (r2 >= c2)
    eye = jnp.where(r2 == c2, 1.0, 0.0).astype(F32)
    lane = lax.broadcasted_iota(jnp.int32, (L, LANES), 1)
    head0 = lane < RWKV_HEAD_DIM
    hr = lax.broadcasted_iota(jnp.int32, (LANES, LANES), 0) // RWKV_HEAD_DIM
    hc = lax.broadcasted_iota(jnp.int32, (LANES, LANES), 1) // RWKV_HEAD_DIM
    head_ones = jnp.where(hr == hc, 1.0, 0.0).astype(BF16)

    def stack(q):
        return jnp.concatenate([jnp.where(head0, q, 0.0), jnp.where(head0, 0.0, q)], axis=0).astype(BF16)

    r_all = r_ref[0]
    k_all = k_ref[0]
    zw = jnp.dot(tw_ref[0], wd2_ref[...], preferred_element_type=F32)
    za = jnp.dot(ta_ref[0], wa2_ref[...], preferred_element_type=F32)
    lw_all = -math.exp(-0.5) * jax.nn.sigmoid(w0_ref[...] + zw)
    asig = jax.nn.sigmoid(a0_ref[...] + za)
    kkr = k_all * kk_ref[...]
    kkr2 = kkr * kkr
    n2 = jnp.concatenate([_head_sum(kkr2[:, p * LANES:(p + 1) * LANES], head_ones) for p in range(pairs)],
                         axis=1)
    kk = kkr / jnp.maximum(jnp.sqrt(n2), 1e-12)
    kd = k_all * (1.0 + (asig - 1.0) * ka_ref[...])
    b_all = kk * asig
    cum = _split_dot(tri, lw_all, 3)
    tot_all = jnp.sum(lw_all, axis=0, keepdims=True)
    at_all = -kk * jnp.exp(cum - lw_all)
    rt_all = r_all * jnp.exp(cum)
    pinv = jnp.exp(-cum)
    pend = jnp.exp(tot_all - cum)
    bi_all, ki_all = b_all * pinv, kd * pinv
    be_all, ke_all = b_all * pend, kd * pend
    decay_all = jnp.exp(tot_all)

    P = range(pairs)
    lanes = [slice(p * LANES, (p + 1) * LANES) for p in P]
    dotf = lambda a, b, dims: lax.dot_general(a, b, dims, preferred_element_type=F32)
    As = [stack(at_all[:, sl]) for sl in lanes]
    Bs = [stack(bi_all[:, sl]) for sl in lanes]
    m_ab = [jnp.where(strict, dotf(As[p], Bs[p], NT), 0.0) for p in P]

    inv = None
    s = 1
    while s < L:
        sh = s.bit_length() - 1
        off_diag = ((r2 >> (sh + 1)) == (c2 >> (sh + 1))) & ((r2 >> sh) != (c2 >> sh))
        e = [jnp.where(off_diag, m_ab[p], 0.0) for p in P]
        if s == 1:
            inv = [eye + e[p] for p in P]
        else:
            invb = [inv[p].astype(BF16) for p in P]
            t = [dotf(e[p].astype(BF16), invb[p], NN).astype(BF16) for p in P]
            inv = [inv[p] + dotf(invb[p], t[p], NN) for p in P]
        s *= 2

    Rs = [stack(rt_all[:, sl]) for sl in lanes]
    Ks = [stack(ki_all[:, sl]) for sl in lanes]
    Vs = [stack(v_ref[0, :, sl]) for sl in lanes]
    m_ak = [jnp.where(strict, dotf(As[p], Ks[p], NT), 0.0).astype(BF16) for p in P]
    m_rb = [jnp.where(incl, dotf(Rs[p], Bs[p], NT), 0.0).astype(BF16) for p in P]
    m_rk = [jnp.where(incl, dotf(Rs[p], Ks[p], NT), 0.0).astype(BF16) for p in P]
    state = [s_scr[p] for p in P]
    sb = [state[p].astype(BF16) for p in P]
    w_s = [dotf(As[p], sb[p], NT) + dotf(m_ak[p], Vs[p], NN) for p in P]
    u_s = [dotf(inv[p].astype(BF16), w_s[p].astype(BF16), NN).astype(BF16) for p in P]
    o_s = [dotf(Rs[p], sb[p], NT) + dotf(m_rb[p], u_s[p], NN) + dotf(m_rk[p], Vs[p], NN) for p in P]
    Bh = [stack(be_all[:, sl]) for sl in lanes]
    Kh = [stack(ke_all[:, sl]) for sl in lanes]
    new_state = [state[p] * decay_all[:, lanes[p]] + dotf(u_s[p], Bh[p], TN) + dotf(Vs[p], Kh[p], TN)
                 for p in P]
    o_ref[0] = jnp.concatenate([o[:L] + o[L:] for o in o_s], axis=1)
    s_scr[...] = jnp.stack(new_state)


def _head_sum(x, head_ones):
    hi = x.astype(BF16)
    lo = (x - hi.astype(F32)).astype(BF16)
    return (lax.dot_general(hi, head_ones, NN, preferred_element_type=F32)
            + lax.dot_general(lo, head_ones, NN, preferred_element_type=F32))


def _wkv(r, k, v, tw, wd2, ta, wa2, w0, a0, k_k, k_a, *, direction, pairs=16):
    B, T, C = r.shape
    L = WKV_CHUNK
    pairs = min(pairs, C // LANES)
    wl = pairs * LANES
    nc = T // L
    R = wd2.shape[0]
    reverse = direction == 1

    def chunk(c):
        return (nc - 1 - c) if reverse else c

    seq = pl.BlockSpec((1, L, wl), lambda b, g, c: (b, chunk(c), g))
    low = pl.BlockSpec((1, L, R), lambda b, g, c: (b, chunk(c), direction))
    fac = pl.BlockSpec((R, wl), lambda b, g, c: (0, g))
    par = pl.BlockSpec((1, wl), lambda b, g, c: (0, g))
    return pl.pallas_call(
        functools.partial(_wkv_body, reverse=reverse, L=L, pairs=pairs),
        out_shape=jax.ShapeDtypeStruct((B, T, C), F32),
        grid=(B, C // wl, nc),
        in_specs=[seq, seq, seq, low, fac, low, fac, par, par, par, par],
        out_specs=seq,
        scratch_shapes=[pltpu.VMEM((pairs, LANES, LANES), F32)],
        compiler_params=_params("parallel", "parallel", "arbitrary"),
        name="wkv7_rev" if reverse else "wkv7_fwd",
    )(r, k, v, tw, wd2, ta, wa2, w0, a0, k_k, k_a)


def _wkv_out_body(of_ref, ob_ref, r_ref, k_ref, v_ref, ta_ref, wa20_ref, wa21_ref, g_ref,
                  a00_ref, a01_ref, ka_ref, rk_ref, lw_ref, lb_ref, o_ref):
    hr = lax.broadcasted_iota(jnp.int32, (LANES, LANES), 0) // RWKV_HEAD_DIM
    hc = lax.broadcasted_iota(jnp.int32, (LANES, LANES), 1) // RWKV_HEAD_DIM
    head_ones = jnp.where(hr == hc, 1.0, 0.0).astype(BF16)
    inv_n = 1.0 / RWKV_HEAD_DIM
    R = wa20_ref.shape[0]
    za0 = jnp.dot(ta_ref[:, :R], wa20_ref[...], preferred_element_type=F32)
    za1 = jnp.dot(ta_ref[:, R:], wa21_ref[...], preferred_element_type=F32)
    asum_all = jax.nn.sigmoid(a00_ref[...] + za0) + jax.nn.sigmoid(a01_ref[...] + za1)
    for t in range(o_ref.shape[1] // LANES):
        sl = slice(t * LANES, (t + 1) * LANES)
        o = of_ref[:, sl] + ob_ref[:, sl]
        mu = _head_sum(o, head_ones) * inv_n
        d = o - mu
        var = _head_sum(d * d, head_ones) * inv_n
        on = d * lax.rsqrt(var + RWKV_GN_EPS) * lw_ref[:, sl] + lb_ref[:, sl]
        asum = asum_all[:, sl]
        ksum = k_ref[:, sl] * (2.0 + (asum - 2.0) * ka_ref[:, sl])
        bonus = _head_sum(r_ref[:, sl] * ksum * rk_ref[:, sl], head_ones) * v_ref[:, sl]
        o_ref[:, sl] = ((on + bonus) * g_ref[:, sl]).astype(o_ref.dtype)


def _wkv_out(o_f, o_b, r, k, v, ta, wa2, g, a0, k_a, r_k, lnx_w, lnx_b):
    M, C = r.shape
    tm = _blk(M, 256, 8)
    tc = _blk(C, 512, LANES)
    R = wa2[0].shape[0]
    seq = pl.BlockSpec((tm, tc), lambda i, j: (i, j))
    low = pl.BlockSpec((tm, 2 * R), lambda i, j: (i, 0))
    fac = pl.BlockSpec((R, tc), lambda i, j: (0, j))
    par = pl.BlockSpec((1, tc), lambda i, j: (0, j))
    row = lambda x: x.reshape(1, C).astype(F32)
    return pl.pallas_call(
        _wkv_out_body,
        out_shape=jax.ShapeDtypeStruct((M, C), BF16),
        grid=(M // tm, C // tc),
        in_specs=[seq] * 5 + [low, fac, fac, seq] + [par] * 6,
        out_specs=seq,
        compiler_params=_params("parallel", "parallel"),
        name="wkv_out",
    )(o_f, o_b, r, k, v, ta, wa2[0], wa2[1], g,
      row(a0[0]), row(a0[1]), row(k_a), row(r_k), row(lnx_w), row(lnx_b))


def _ffn(x2, p):
    h = _rmsnorm(x2, p["norm_ffn"], out_dtype=BF16)
    hidden = p["w_down"].shape[0]
    mid = _swiglu_up(h, p["w_gu"], hidden)
    return _matmul_residual(mid, p["w_down"], x2, tm=512, tn=256, name="ffn_down")


def _rwkv_layer(x, p, v_first):
    B, T, C = x.shape
    M = B * T
    x2 = x.reshape(M, C)
    h = _rmsnorm(x2, p["norm_mix"]).reshape(B, T, C)
    hp = jnp.pad(h, ((0, 0), (1, 1), (0, 0)))
    dx = 0.5 * (hp[:, :-2] + hp[:, 2:]) - h
    mix = p["mix"]
    xr, xw, xk, xv, xa, xg = ((h + dx * mix[i]).astype(BF16).reshape(M, C) for i in range(6))
    r = _matmul(xr, p["w_r"], out_dtype=F32, name="rwkv_r")
    k = _matmul(xk, p["w_k"], out_dtype=F32, name="rwkv_k")
    v = _matmul(xv, p["w_v"], out_dtype=F32, name="rwkv_v")
    tw = _matmul(xw, p["decay1"], out_dtype=BF16, epilogue=functools.partial(_epi_plain, act="tanh"),
                 name="rwkv_decay1")
    ta = _matmul(xa, p["a1"], out_dtype=BF16, name="rwkv_a1")
    if v_first is not None:
        tv = _matmul(xv, p["v1"], out_dtype=BF16, name="rwkv_v1")
        tm_v, tn_v = _mm_tiles(M, p["v2"].shape[0], C, None, None)
        tile = pl.BlockSpec((tm_v, tn_v), lambda i, j: (i, j))
        v = _matmul(tv, p["v2"], out_dtype=F32, epilogue=_epi_value_residual,
                    extra=(v, v_first, p["v0"].reshape(1, C).astype(F32)),
                    extra_specs=(tile, tile, pl.BlockSpec((1, tn_v), lambda i, j: (0, j))), name="rwkv_v2")
    tg = _matmul(xg, p["g1"], out_dtype=BF16, epilogue=functools.partial(_epi_plain, act="sigmoid"),
                 name="rwkv_g1")
    g = _matmul(tg, p["g2"], out_dtype=F32, name="rwkv_g2")
    seq = lambda t: t.reshape(B, T, C)
    row = lambda t: t.reshape(1, C).astype(F32)
    low = lambda t: t.reshape(B, T, t.shape[1])
    o_dir = [
        _wkv(seq(r), seq(k), seq(v), low(tw), p["decay2"][d], low(ta), p["a2"][d],
             row(p["w_decay0"][d]), row(p["a0"][d]), row(p["k_k"]), row(p["k_a"]), direction=d)
        for d in range(2)
    ]
    y = _wkv_out(o_dir[0].reshape(M, C), o_dir[1].reshape(M, C), r, k, v, ta, p["a2"], g,
                 p["a0"], p["k_a"], p["r_k"], p["lnx_w"], p["lnx_b"])
    x2 = _matmul_residual(y, p["w_o"], x2, name="rwkv_o")
    return x2, v


def _mla_layer(x, p):
    B, T, C = x.shape
    M = B * T
    x2 = x.reshape(M, C)
    H = p["w_uq"].shape[1] // MLA_QK_PAD
    q_rank = p["q_norm"].shape[0]
    kv_rank = p["kv_norm"].shape[0]
    h = _rmsnorm(x2, p["norm_mix"], out_dtype=BF16)
    lat = _matmul(h, p["w_in"], out_dtype=F32, tn=256, name="mla_in")
    c_q = _rmsnorm(lat, p["q_norm"], out_dtype=BF16, width=q_rank, col_block=0)
    c_kv = _rmsnorm(lat, p["kv_norm"], out_dtype=BF16, width=kv_rank, col_block=q_rank // kv_rank)
    tables = _rope_tables(T, MLA_ROPE_DIM)
    half = MLA_ROPE_DIM // 2
    k_rope = _rope_cols(lat, tables, T, (q_rank + kv_rank) // LANES, half)
    tm = _blk(T, 1024, 8)
    qscale = (MLA_NOPE_DIM + MLA_ROPE_DIM) ** -0.5 * math.log2(math.e)
    q = _matmul(c_q, p["w_uq"], out_dtype=BF16, tm=tm, tn=_blk(H * MLA_QK_PAD, 1024, MLA_QK_PAD),
                epilogue=functools.partial(_epi_rope_second_tile, half=half, qscale=qscale),
                extra=tables, extra_specs=_table_specs(T, _blk(M, tm, 8)), name="mla_uq")
    tm_kv = _blk(M, 1024, 8)
    tn_kv = _blk(H * MLA_V_DIM, 2048, MLA_QK_PAD)
    kv = _matmul(c_kv, p["w_ukv"], out_dtype=BF16, tm=tm_kv, tn=tn_kv,
                 epilogue=functools.partial(_epi_add_second_tile, n_add_blocks=H * MLA_QK_PAD // tn_kv),
                 extra=(k_rope,), extra_specs=(pl.BlockSpec((tm_kv, LANES), lambda i, j: (i, 0)),),
                 name="mla_ukv")
    q3 = q.reshape(B, T, H * MLA_QK_PAD)
    kv3 = kv.reshape(B, T, kv.shape[1])
    o = _flash(q3, kv3, kv3[:, :, H * MLA_QK_PAD:], heads=H, n_maps=1, dqk=MLA_QK_PAD, dv=MLA_V_DIM,
               q_blk0=0, k_blk0=0)
    return _matmul_residual(o.reshape(M, H * MLA_V_DIM), p["w_o"], x2, name="mla_o")


def _diff_layer(x, p, lambda_init):
    B, T, C = x.shape
    M = B * T
    x2 = x.reshape(M, C)
    H = C // (2 * DIFF_HEAD_DIM)
    h = _rmsnorm(x2, p["norm_mix"], out_dtype=BF16)
    tables = _rope_tables(T, DIFF_ROT_DIM)
    tm = _blk(T, 1024, 8)
    tn = _blk(C, 512, LANES)
    qkv = _matmul(h, p["w_qkv"], out_dtype=BF16, tm=tm, tn=tn,
                  epilogue=functools.partial(_epi_rope_tiles, half=DIFF_ROT_DIM // 2,
                                             n_rope_blocks=2 * C // tn, n_q_blocks=C // tn,
                                             qscale=DIFF_HEAD_DIM ** -0.5 * math.log2(math.e)),
                  extra=tables, extra_specs=_table_specs(T, _blk(M, tm, 8)), name="diff_qkv")
    qkv3 = qkv.reshape(B, T, 3 * C)
    o = _flash(qkv3, qkv3, qkv3[:, :, 2 * C:], heads=H, n_maps=2, dqk=DIFF_HEAD_DIM,
               dv=2 * DIFF_HEAD_DIM, q_blk0=0, k_blk0=H,
               diff_params=(p["lambda"].astype(F32), p["subln"].reshape(1, -1).astype(F32)),
               lambda_init=lambda_init)
    return _matmul_residual(o.reshape(M, C), p["w_o"], x2, name="diff_o")


def _trunk(x, layers, final_norm):
    B, T, C = x.shape
    v_first = None
    for i, p in enumerate(layers):
        kind = p["kind"]
        if kind == "rwkv":
            x2, v_layer = _rwkv_layer(x, p, v_first)
            if v_first is None:
                v_first = v_layer
        elif kind == "mla":
            x2 = _mla_layer(x, p)
        else:
            x2 = _diff_layer(x, p, 0.8 - 0.6 * math.exp(-0.3 * i))
        x2 = _ffn(x2, p)
        x = x2.reshape(B, T, C)
    return _rmsnorm(x.reshape(B * T, C), final_norm).reshape(B, T, C)


def _prep_ffn(p):
    return {"norm_ffn": p["norm_ffn"], "w_gu": p["w_gu"].astype(BF16), "w_down": p["w_down"].astype(BF16)}


def _prep_rwkv(p):
    b = lambda w: w.astype(BF16)
    out = dict(kind="rwkv", norm_mix=p["norm_mix"], mix=p["mix"],
               w_r=b(p["w_rkv"][0]), w_k=b(p["w_rkv"][1]), w_v=b(p["w_rkv"][2]),
               w_decay0=p["w_decay0"], a0=p["a0"], k_k=p["k_k"], k_a=p["k_a"], r_k=p["r_k"],
               lnx_w=p["lnx_w"], lnx_b=p["lnx_b"], w_o=b(p["w_o"]))
    for name1, name2, key1, key2 in (("w_decay1", "w_decay2", "decay1", "decay2"), ("a1", "a2", "a1", "a2")):
        w1 = [_pad_to(p[name1][d], 1, LANES) for d in range(2)]
        out[key1] = b(jnp.concatenate(w1, axis=1))
        out[key2] = [b(_pad_to(p[name2][d], 0, LANES)) for d in range(2)]
    if "v0" in p:
        out["v0"] = p["v0"]
        out["v1"] = b(_pad_to(p["v1"], 1, LANES))
        out["v2"] = b(_pad_to(p["v2"], 0, LANES))
    out["g1"] = b(_pad_to(p["g1"], 1, LANES))
    out["g2"] = b(_pad_to(p["g2"], 0, LANES))
    out.update(_prep_ffn(p))
    return out


def _prep_mla(p):
    b = lambda w: w.astype(BF16)
    q_rank = p["q_norm"].shape[0]
    kv_rank = p["kv_norm"].shape[0]
    qk = MLA_NOPE_DIM + MLA_ROPE_DIM
    H = p["w_uq"].shape[1] // qk
    w_uq = _pad_to(p["w_uq"].reshape(q_rank, H, qk), 2, MLA_QK_PAD).reshape(q_rank, H * MLA_QK_PAD)
    w_ukv = p["w_ukv"].reshape(kv_rank, H, MLA_NOPE_DIM + MLA_V_DIM)
    w_k = _pad_to(w_ukv[:, :, :MLA_NOPE_DIM], 2, MLA_QK_PAD).reshape(kv_rank, H * MLA_QK_PAD)
    w_v = w_ukv[:, :, MLA_NOPE_DIM:].reshape(kv_rank, H * MLA_V_DIM)
    out = dict(kind="mla", norm_mix=p["norm_mix"], w_in=b(_pad_to(p["w_in"], 1, 256)),
               q_norm=p["q_norm"], kv_norm=p["kv_norm"], w_uq=b(w_uq),
               w_ukv=b(jnp.concatenate([w_k, w_v], axis=1)), w_o=b(p["w_o"]))
    out.update(_prep_ffn(p))
    return out


def _prep_diff(p):
    out = dict(kind="diff", norm_mix=p["norm_mix"], w_qkv=p["w_qkv"].astype(BF16), subln=p["subln"],
               w_o=p["w_o"].astype(BF16))
    out["lambda"] = p["lambda"]
    out.update(_prep_ffn(p))
    return out


def _forward(x_prompt, x_sample, raw_layers, final_norm):
    prep = {"rwkv": _prep_rwkv, "mla": _prep_mla, "diff": _prep_diff}
    kinds = ("rwkv", "mla", "diff")
    layers = [prep[kinds[i % 3]](p) for i, p in enumerate(raw_layers)]
    return (_trunk(x_prompt, layers, final_norm), _trunk(x_sample, layers, final_norm))


def kernel(x_prompt, x_sample, l0_norm_mix, l0_mix, l0_w_rkv, l0_w_decay0, l0_w_decay1, l0_w_decay2, l0_a0, l0_a1, l0_a2, l0_g1, l0_g2, l0_k_k, l0_k_a, l0_r_k, l0_lnx_w, l0_lnx_b, l0_w_o, l0_norm_ffn, l0_w_gu, l0_w_down, l1_norm_mix, l1_w_in, l1_q_norm, l1_kv_norm, l1_w_uq, l1_w_ukv, l1_w_o, l1_norm_ffn, l1_w_gu, l1_w_down, l2_norm_mix, l2_w_qkv, l2_lambda, l2_subln, l2_w_o, l2_norm_ffn, l2_w_gu, l2_w_down, l3_norm_mix, l3_mix, l3_w_rkv, l3_w_decay0, l3_w_decay1, l3_w_decay2, l3_a0, l3_a1, l3_a2, l3_v0, l3_v1, l3_v2, l3_g1, l3_g2, l3_k_k, l3_k_a, l3_r_k, l3_lnx_w, l3_lnx_b, l3_w_o, l3_norm_ffn, l3_w_gu, l3_w_down, final_norm):
    raw_layers = (
        dict(norm_mix=l0_norm_mix, mix=l0_mix, w_rkv=l0_w_rkv, w_decay0=l0_w_decay0, w_decay1=l0_w_decay1,
             w_decay2=l0_w_decay2, a0=l0_a0, a1=l0_a1, a2=l0_a2, g1=l0_g1, g2=l0_g2, k_k=l0_k_k, k_a=l0_k_a,
             r_k=l0_r_k, lnx_w=l0_lnx_w, lnx_b=l0_lnx_b, w_o=l0_w_o,
             norm_ffn=l0_norm_ffn, w_gu=l0_w_gu, w_down=l0_w_down),
        dict(norm_mix=l1_norm_mix, w_in=l1_w_in, q_norm=l1_q_norm, kv_norm=l1_kv_norm, w_uq=l1_w_uq,
             w_ukv=l1_w_ukv, w_o=l1_w_o, norm_ffn=l1_norm_ffn, w_gu=l1_w_gu, w_down=l1_w_down),
        dict(norm_mix=l2_norm_mix, w_qkv=l2_w_qkv, subln=l2_subln, w_o=l2_w_o,
             norm_ffn=l2_norm_ffn, w_gu=l2_w_gu, w_down=l2_w_down, **{"lambda": l2_lambda}),
        dict(norm_mix=l3_norm_mix, mix=l3_mix, w_rkv=l3_w_rkv, w_decay0=l3_w_decay0, w_decay1=l3_w_decay1,
             w_decay2=l3_w_decay2, a0=l3_a0, a1=l3_a1, a2=l3_a2, v0=l3_v0, v1=l3_v1, v2=l3_v2,
             g1=l3_g1, g2=l3_g2, k_k=l3_k_k, k_a=l3_k_a, r_k=l3_r_k, lnx_w=l3_lnx_w, lnx_b=l3_lnx_b, w_o=l3_w_o,
             norm_ffn=l3_norm_ffn, w_gu=l3_w_gu, w_down=l3_w_down),
    )
    return _forward(x_prompt, x_sample, raw_layers, final_norm)
```

```python
import functools
import math

import jax
import jax.numpy as jnp
from jax import lax
from jax.experimental import pallas as pl
from jax.experimental.pallas import tpu as pltpu

F32 = jnp.float32
BF16 = jnp.bfloat16

LANES = 128
VMEM_LIMIT = 56 * 1024 * 1024

RMS_EPS = 1e-6
ROPE_THETA = 500000.0
RWKV_HEAD_DIM = 64
RWKV_GN_EPS = 64e-5
WKV_CHUNK = 64
MLA_NOPE_DIM = 128
MLA_ROPE_DIM = 64
MLA_V_DIM = 128
MLA_QK_PAD = 256
DIFF_HEAD_DIM = 128
DIFF_ROT_DIM = DIFF_HEAD_DIM // 4
DIFF_EPS = 1e-5
DEPTH = 4

NN = (((1,), (0,)), ((), ()))
NT = (((1,), (1,)), ((), ()))
TN = (((0,), (0,)), ((), ()))


def _blk(n, pref, align):
    if n <= pref:
        return n
    d = (pref // align) * align
    while d >= align:
        if n % d == 0:
            return d
        d -= align
    raise ValueError(f"no block for {n} (pref {pref}, align {align})")


def _params(*sem):
    return pltpu.CompilerParams(dimension_semantics=sem, vmem_limit_bytes=VMEM_LIMIT)


def _pad_to(x, axis, mult):
    n = x.shape[axis]
    pad = (-n) % mult
    if pad == 0:
        return x
    widths = [(0, 0)] * x.ndim
    widths[axis] = (0, pad)
    return jnp.pad(x, widths)


def _rmsnorm_body(x_ref, g_ref, o_ref, *, eps):
    x = x_ref[...].astype(F32)
    y = x * lax.rsqrt(jnp.mean(x * x, axis=-1, keepdims=True) + eps)
    o_ref[...] = (y * g_ref[...]).astype(o_ref.dtype)


def _rmsnorm(x, g, *, eps=RMS_EPS, out_dtype=F32, width=None, col_block=0):
    M = x.shape[0]
    width = x.shape[1] if width is None else width
    tm = _blk(M, 512, 8)
    return pl.pallas_call(
        functools.partial(_rmsnorm_body, eps=eps),
        out_shape=jax.ShapeDtypeStruct((M, width), out_dtype),
        grid=(M // tm,),
        in_specs=[pl.BlockSpec((tm, width), lambda i: (i, col_block)),
                  pl.BlockSpec((1, width), lambda i: (0, 0))],
        out_specs=pl.BlockSpec((tm, width), lambda i: (i, 0)),
        compiler_params=_params("parallel"),
        name="rmsnorm",
    )(x, g.reshape(1, width).astype(F32))


def _premix_body(x_ref, prev_ref, next_ref, g_ref, mix_ref, *o_refs, eps):
    i = pl.program_id(1)
    g = g_ref[...]

    def norm(v):
        return v * lax.rsqrt(jnp.mean(v * v, axis=-1, keepdims=True) + eps) * g

    h = norm(x_ref[0])
    tt = h.shape[0]
    h_before = jnp.where(i == 0, 0.0, norm(prev_ref[0])[7:8])
    h_after = jnp.where(i == pl.num_programs(1) - 1, 0.0, norm(next_ref[0])[0:1])
    row = lax.broadcasted_iota(jnp.int32, (tt, 1), 0)
    h_dn = jnp.where(row == 0, h_before, pltpu.roll(h, 1, 0))
    h_up = jnp.where(row == tt - 1, h_after, pltpu.roll(h, tt - 1, 0))
    dx = 0.5 * (h_dn + h_up) - h
    for n, o_ref in enumerate(o_refs):
        o_ref[0] = (h + dx * mix_ref[n:n + 1, :]).astype(o_ref.dtype)


def _premix(x, g, mix, *, tt=256):
    B, T, C = x.shape
    tt = _blk(T, tt, 8)
    nb = tt // 8
    last = T // 8 - 1
    n_mix = mix.shape[0]
    out = jax.ShapeDtypeStruct((B, T, C), BF16)
    tile = pl.BlockSpec((1, tt, C), lambda b, i: (b, i, 0))
    return pl.pallas_call(
        functools.partial(_premix_body, eps=RMS_EPS),
        out_shape=[out] * n_mix,
        grid=(B, T // tt),
        in_specs=[tile,
                  pl.BlockSpec((1, 8, C), lambda b, i: (b, jnp.maximum(i * nb - 1, 0), 0)),
                  pl.BlockSpec((1, 8, C), lambda b, i: (b, jnp.minimum((i + 1) * nb, last), 0)),
                  pl.BlockSpec((1, C), lambda b, i: (0, 0)),
                  pl.BlockSpec((n_mix, C), lambda b, i: (0, 0))],
        out_specs=[tile] * n_mix,
        compiler_params=_params("parallel", "parallel"),
        name="rwkv_premix",
    )(x, x, x, g.reshape(1, C).astype(F32), mix.astype(F32))


def _rope_tables(T, dim):
    half = dim // 2
    inv_freq = 1.0 / (ROPE_THETA ** (jnp.arange(0, dim, 2, dtype=F32) / dim))
    ang = jnp.arange(T, dtype=F32)[:, None] * inv_freq[None, :]
    c, s = jnp.cos(ang), jnp.sin(ang)
    ones = jnp.ones((T, LANES - dim), F32)
    zeros = jnp.zeros((T, LANES - dim), F32)
    zh = jnp.zeros((T, half), F32)
    C = jnp.concatenate([c, c, ones], axis=1)
    S1 = jnp.concatenate([-s, zh, zeros], axis=1)
    S2 = jnp.concatenate([zh, s, zeros], axis=1)
    return C, S1, S2


def _rope_tile(t, c, s1, s2, half):
    return t * c + pltpu.roll(t, LANES - half, 1) * s1 + pltpu.roll(t, half, 1) * s2


def _rope_cols_body(x_ref, c_ref, s1_ref, s2_ref, o_ref, *, half):
    o_ref[...] = _rope_tile(x_ref[...].astype(F32), c_ref[...], s1_ref[...], s2_ref[...], half)


def _rope_cols(x, tables, T, col_block, half):
    M = x.shape[0]
    tm = _blk(T, 512, 8)
    nt = T // tm
    tab = pl.BlockSpec((tm, LANES), lambda i: (i % nt, 0))
    return pl.pallas_call(
        functools.partial(_rope_cols_body, half=half),
        out_shape=jax.ShapeDtypeStruct((M, LANES), F32),
        grid=(M // tm,),
        in_specs=[pl.BlockSpec((tm, LANES), lambda i: (i, col_block)), tab, tab, tab],
        out_specs=pl.BlockSpec((tm, LANES), lambda i: (i, 0)),
        compiler_params=_params("parallel"),
        name="rope_cols",
    )(x, *tables)


def _mm_body(*refs, n_extra, epilogue):
    x_ref, w_ref = refs[0], refs[1]
    extra = refs[2:2 + n_extra]
    o_ref = refs[2 + n_extra]
    acc = jnp.dot(x_ref[...], w_ref[...], preferred_element_type=F32)
    epilogue(acc, extra, o_ref)


def _epi_plain(acc, extra, o_ref, *, act):
    if act == "tanh":
        acc = jnp.tanh(acc)
    elif act == "sigmoid":
        acc = jax.nn.sigmoid(acc)
    o_ref[...] = acc.astype(o_ref.dtype)


def _epi_residual(acc, extra, o_ref):
    o_ref[...] = (extra[0][...] + acc).astype(o_ref.dtype)


def _epi_value_residual(acc, extra, o_ref):
    v = extra[0][...]
    o_ref[...] = (v + (extra[1][...] - v) * jax.nn.sigmoid(extra[2][...] + acc)).astype(o_ref.dtype)


def _epi_rope_second_tile(acc, extra, o_ref, *, half, qscale):
    c, s1, s2 = (e[...] for e in extra)
    acc = acc * qscale
    for t in range(acc.shape[1] // LANES):
        sl = slice(t * LANES, (t + 1) * LANES)
        tile = _rope_tile(acc[:, sl], c, s1, s2, half) if t % 2 else acc[:, sl]
        o_ref[:, sl] = tile.astype(o_ref.dtype)


def _epi_rope_tiles(acc, extra, o_ref, *, half, n_rope_blocks, n_q_blocks, qscale):
    j = pl.program_id(1)

    @pl.when(j < n_rope_blocks)
    def _():
        c, s1, s2 = (e[...] for e in extra)
        scaled = acc * jnp.where(j < n_q_blocks, qscale, 1.0).astype(F32)
        for t in range(acc.shape[1] // LANES):
            sl = slice(t * LANES, (t + 1) * LANES)
            o_ref[:, sl] = _rope_tile(scaled[:, sl], c, s1, s2, half).astype(o_ref.dtype)

    @pl.when(j >= n_rope_blocks)
    def _():
        o_ref[...] = acc.astype(o_ref.dtype)


def _epi_add_second_tile(acc, extra, o_ref, *, n_add_blocks):
    j = pl.program_id(1)

    @pl.when(j < n_add_blocks)
    def _():
        shared = extra[0][...]
        for t in range(acc.shape[1] // LANES):
            sl = slice(t * LANES, (t + 1) * LANES)
            tile = acc[:, sl] + shared if t % 2 else acc[:, sl]
            o_ref[:, sl] = tile.astype(o_ref.dtype)

    @pl.when(j >= n_add_blocks)
    def _():
        o_ref[...] = acc.astype(o_ref.dtype)


X_TILE_BYTES = 8 * 1024 * 1024


def _mm_tiles(M, K, N, tm, tn):
    if tm is None:
        tm = 1024 if 1024 * K * 2 <= X_TILE_BYTES else 512
    if tn is None:
        tn = 512
    return _blk(M, tm, 8), _blk(N, tn, LANES)


def _matmul(x, w, *, out_dtype, epilogue=None, extra=(), extra_specs=(), tm=None, tn=None,
            name="matmul"):
    M = x.shape[0]
    K, N = w.shape
    tm, tn = _mm_tiles(M, K, N, tm, tn)
    if epilogue is None:
        epilogue = functools.partial(_epi_plain, act=None)
    x_mode = pl.Buffered(1) if tm * K * x.dtype.itemsize > X_TILE_BYTES else None
    return pl.pallas_call(
        functools.partial(_mm_body, n_extra=len(extra), epilogue=epilogue),
        out_shape=jax.ShapeDtypeStruct((M, N), out_dtype),
        grid=(M // tm, N // tn),
        in_specs=[pl.BlockSpec((tm, K), lambda i, j: (i, 0), pipeline_mode=x_mode),
                  pl.BlockSpec((K, tn), lambda i, j: (0, j))] + list(extra_specs),
        out_specs=pl.BlockSpec((tm, tn), lambda i, j: (i, j)),
        compiler_params=_params("parallel", "arbitrary"),
        name=name,
    )(x, w, *extra)


def _matmul_residual(x, w, res, *, tm=None, tn=None, name="matmul_residual"):
    M = x.shape[0]
    K, N = w.shape
    tm_, tn_ = _mm_tiles(M, K, N, tm, tn)
    return _matmul(x, w, out_dtype=F32, epilogue=_epi_residual, extra=(res,),
                   extra_specs=(pl.BlockSpec((tm_, tn_), lambda i, j: (i, j)),), tm=tm, tn=tn, name=name)


def _table_specs(T, tm):
    nt = T // tm
    return tuple(pl.BlockSpec((tm, LANES), lambda i, j: (i % nt, 0)) for _ in range(3))


def _swiglu_body(x_ref, wg_ref, wu_ref, o_ref):
    x = x_ref[...]
    g = jnp.dot(x, wg_ref[...], preferred_element_type=F32)
    u = jnp.dot(x, wu_ref[...], preferred_element_type=F32)
    o_ref[...] = (g * jax.nn.sigmoid(g) * u).astype(o_ref.dtype)


def _swiglu_up(x, w_gu, hidden, *, tm=2048, tn=256):
    M, K = x.shape
    tm = _blk(M, tm, 8)
    tn = _blk(hidden, tn, LANES)
    nb = hidden // tn
    return pl.pallas_call(
        _swiglu_body,
        out_shape=jax.ShapeDtypeStruct((M, hidden), BF16),
        grid=(M // tm, nb),
        in_specs=[pl.BlockSpec((tm, K), lambda i, j: (i, 0), pipeline_mode=pl.Buffered(1)),
                  pl.BlockSpec((K, tn), lambda i, j: (0, j)),
                  pl.BlockSpec((K, tn), lambda i, j: (0, j + nb))],
        out_specs=pl.BlockSpec((tm, tn), lambda i, j: (i, j)),
        compiler_params=_params("parallel", "arbitrary"),
        name="swiglu_up",
    )(x, w_gu, w_gu)


def _flash_body(*refs, n_maps, dqk, dv, diff, lambda_init, bks):
    if diff:
        q_ref, k_ref, vt_ref, lam_ref, g_ref, o_ref, m_scr, acc_scr = refs
    else:
        q_ref, k_ref, vt_ref, o_ref, m_scr, acc_scr = refs
    j = pl.program_id(3)

    @pl.when(j == 0)
    def _():
        m_scr[...] = jnp.full(m_scr.shape, -jnp.inf, F32)
        acc_scr[...] = jnp.zeros(acc_scr.shape, F32)

    q = q_ref[0]
    maps = [slice(mp * dqk, (mp + 1) * dqk) for mp in range(n_maps)]
    m = [m_scr[mp] for mp in range(n_maps)]
    acc = [acc_scr[mp] for mp in range(n_maps)]
    n_sub = k_ref.shape[1] // bks
    st = []
    for sub in range(n_sub):
        k = k_ref[0, sub * bks:(sub + 1) * bks, :]
        st.append([lax.dot_general(k[:, sl], q[:, sl], NT, preferred_element_type=F32) for sl in maps])
    for sub in range(n_sub):
        vt = vt_ref[0, :, sub * bks:(sub + 1) * bks]
        for mp in range(n_maps):
            s_t = st[sub][mp]
            m_new = jnp.maximum(m[mp], jnp.max(s_t, axis=0, keepdims=True))
            alpha = jnp.exp2(m[mp] - m_new)
            p = jnp.exp2(s_t - m_new).astype(vt.dtype)
            acc[mp] = alpha * acc[mp] + jnp.dot(vt, p, preferred_element_type=F32)
            m[mp] = m_new
    for mp in range(n_maps):
        m_scr[mp] = m[mp]
        acc_scr[mp] = acc[mp]

    @pl.when(j == pl.num_programs(3) - 1)
    def _():
        o = [(a[:dv] / a[dv:dv + 1]).T for a in acc]
        if diff:
            lp = lam_ref[...]
            lam = (jnp.exp(jnp.sum(lp[0:1] * lp[1:2], axis=-1, keepdims=True))
                   - jnp.exp(jnp.sum(lp[2:3] * lp[3:4], axis=-1, keepdims=True)) + lambda_init)
            od = o[0] - lam * o[1]
            y = od * lax.rsqrt(jnp.mean(od * od, axis=-1, keepdims=True) + DIFF_EPS) * g_ref[...]
            o_ref[0] = (y * (1.0 - lambda_init)).astype(o_ref.dtype)
        else:
            o_ref[0] = o[0].astype(o_ref.dtype)


ONES_ROWS = 16


def _v_transposed(v, heads, dv):
    B, T = v.shape[0], v.shape[1]
    vt = jnp.swapaxes(v, 1, 2).reshape(B, heads, dv, T)
    ones = jnp.ones((B, heads, ONES_ROWS, T), v.dtype)
    return jnp.concatenate([vt, ones], axis=2).reshape(B, heads * (dv + ONES_ROWS), T)


def _flash(q_arr, k_arr, v_arr, *, heads, n_maps, dqk, dv, q_blk0, k_blk0,
           diff_params=None, lambda_init=0.0, bq=512, bk=8192, bks=256):
    B, T = q_arr.shape[0], q_arr.shape[1]
    bq = _blk(T, bq, LANES)
    bk = _blk(T, bk, LANES)
    bks = _blk(bk, bks, LANES)
    wqk = n_maps * dqk
    dve = dv + ONES_ROWS
    diff = diff_params is not None
    in_specs = [pl.BlockSpec((1, bq, wqk), lambda b, h, i, j: (b, i, q_blk0 + h)),
                pl.BlockSpec((1, bk, wqk), lambda b, h, i, j: (b, j, k_blk0 + h)),
                pl.BlockSpec((1, dve, bk), lambda b, h, i, j: (b, h, j))]
    args = [q_arr, k_arr, _v_transposed(v_arr, heads, dv)]
    if diff:
        lam, subln = diff_params
        in_specs += [pl.BlockSpec(lam.shape, lambda b, h, i, j: (0, 0)),
                     pl.BlockSpec(subln.shape, lambda b, h, i, j: (0, 0))]
        args += [lam, subln]
    return pl.pallas_call(
        functools.partial(_flash_body, n_maps=n_maps, dqk=dqk, dv=dv, diff=diff,
                          lambda_init=lambda_init, bks=bks),
        out_shape=jax.ShapeDtypeStruct((B, T, heads * dv), BF16),
        grid=(B, heads, T // bq, T // bk),
        in_specs=in_specs,
        out_specs=pl.BlockSpec((1, bq, dv), lambda b, h, i, j: (b, i, h)),
        scratch_shapes=[pltpu.VMEM((n_maps, 1, bq), F32),
                        pltpu.VMEM((n_maps, dve, bq), F32)],
        compiler_params=_params("parallel", "parallel", "parallel", "arbitrary"),
        name="flash_diff" if diff else "flash_mla",
    )(*args)


def _split_dot(m_exact, x, n_parts):
    acc = None
    rem = x
    for _ in range(n_parts):
        part = rem.astype(BF16)
        rem = rem - part.astype(F32)
        d = lax.dot_general(m_exact, part, NN, preferred_element_type=F32)
        acc = d if acc is None else acc + d
    return acc


def _wkv_body(r_ref, k_ref, v_ref, tw_ref, wd2_ref, ta_ref, wa2_ref, w0_ref, a0_ref, kk_ref, ka_ref,
              o_ref, s_scr,
              *, reverse, L, pairs):
    c = pl.program_id(2)

    @pl.when(c == 0)
    def _():
        s_scr[...] = jnp.zeros(s_scr.shape, F32)

    L2 = 2 * L
    ri = lax.broadcasted_iota(jnp.int32, (L, L), 0)
    ci = lax.broadcasted_iota(jnp.int32, (L, L), 1)
    tri = ((ri <= ci) if reverse else (ri >= ci)).astype(BF16)
    r2 = lax.broadcasted_iota(jnp.int32, (L2, L2), 0)
    c2 = lax.broadcasted_iota(jnp.int32, (L2, L2), 1)
    strict = (r2 < c2) if reverse else (r2 > c2)
    incl = (r2 <= c2) if reverse else (r2 >= c2)
    eye = jnp.where(r2 == c2, 1.0, 0.0).astype(F32)
    lane = lax.broadcasted_iota(jnp.int32, (L, LANES), 1)
    head0 = lane < RWKV_HEAD_DIM
    hr = lax.broadcasted_iota(jnp.int32, (LANES, LANES), 0) // RWKV_HEAD_DIM
    hc = lax.broadcasted_iota(jnp.int32, (LANES, LANES), 1) // RWKV_HEAD_DIM
    head_ones = jnp.where(hr == hc, 1.0, 0.0).astype(BF16)

    def stack(q):
        return jnp.concatenate([jnp.where(head0, q, 0.0), jnp.where(head0, 0.0, q)], axis=0).astype(BF16)

    r_all = r_ref[0]
    k_all = k_ref[0]
    zw = jnp.dot(tw_ref[0], wd2_ref[...], preferred_element_type=F32)
    za = jnp.dot(ta_ref[0], wa2_ref[...], preferred_element_type=F32)
    lw_all = -math.exp(-0.5) * jax.nn.sigmoid(w0_ref[...] + zw)
    asig = jax.nn.sigmoid(a0_ref[...] + za)
    kkr = k_all * kk_ref[...]
    kkr2 = kkr * kkr
    n2 = jnp.concatenate([_head_sum(kkr2[:, p * LANES:(p + 1) * LANES], head_ones) for p in range(pairs)],
                         axis=1)
    kk = kkr / jnp.maximum(jnp.sqrt(n2), 1e-12)
    kd = k_all * (1.0 + (asig - 1.0) * ka_ref[...])
    b_all = kk * asig
    cum = _split_dot(tri, lw_all, 3)
    tot_all = jnp.sum(lw_all, axis=0, keepdims=True)
    at_all = -kk * jnp.exp(cum - lw_all)
    rt_all = r_all * jnp.exp(cum)
    pinv = jnp.exp(-cum)
    pend = jnp.exp(tot_all - cum)
    bi_all, ki_all = b_all * pinv, kd * pinv
    be_all, ke_all = b_all * pend, kd * pend
    decay_all = jnp.exp(tot_all)

    P = range(pairs)
    lanes = [slice(p * LANES, (p + 1) * LANES) for p in P]
    dotf = lambda a, b, dims: lax.dot_general(a, b, dims, preferred_element_type=F32)
    As = [stack(at_all[:, sl]) for sl in lanes]
    Bs = [stack(bi_all[:, sl]) for sl in lanes]
    m_ab = [jnp.where(strict, dotf(As[p], Bs[p], NT), 0.0) for p in P]

    inv = None
    s = 1
    while s < L:
        sh = s.bit_length() - 1
        off_diag = ((r2 >> (sh + 1)) == (c2 >> (sh + 1))) & ((r2 >> sh) != (c2 >> sh))
        e = [jnp.where(off_diag, m_ab[p], 0.0) for p in P]
        if s == 1:
            inv = [eye + e[p] for p in P]
        else:
            invb = [inv[p].astype(BF16) for p in P]
            t = [dotf(e[p].astype(BF16), invb[p], NN).astype(BF16) for p in P]
            inv = [inv[p] + dotf(invb[p], t[p], NN) for p in P]
        s *= 2

    Rs = [stack(rt_all[:, sl]) for sl in lanes]
    Ks = [stack(ki_all[:, sl]) for sl in lanes]
    Vs = [stack(v_ref[0, :, sl]) for sl in lanes]
    m_ak = [jnp.where(strict, dotf(As[p], Ks[p], NT), 0.0).astype(BF16) for p in P]
    m_rb = [jnp.where(incl, dotf(Rs[p], Bs[p], NT), 0.0).astype(BF16) for p in P]
    m_rk = [jnp.where(incl, dotf(Rs[p], Ks[p], NT), 0.0).astype(BF16) for p in P]
    state = [s_scr[p] for p in P]
    sb = [state[p].astype(BF16) for p in P]
    w_s = [dotf(As[p], sb[p], NT) + dotf(m_ak[p], Vs[p], NN) for p in P]
    u_s = [dotf(inv[p].astype(BF16), w_s[p].astype(BF16), NN).astype(BF16) for p in P]
    o_s = [dotf(Rs[p], sb[p], NT) + dotf(m_rb[p], u_s[p], NN) + dotf(m_rk[p], Vs[p], NN) for p in P]
    Bh = [stack(be_all[:, sl]) for sl in lanes]
    Kh = [stack(ke_all[:, sl]) for sl in lanes]
    new_state = [state[p] * decay_all[:, lanes[p]] + dotf(u_s[p], Bh[p], TN) + dotf(Vs[p], Kh[p], TN)
                 for p in P]
    o_ref[0] = jnp.concatenate([o[:L] + o[L:] for o in o_s], axis=1)
    s_scr[...] = jnp.stack(new_state)


def _head_sum(x, head_ones):
    hi = x.astype(BF16)
    lo = (x - hi.astype(F32)).astype(BF16)
    return (lax.dot_general(hi, head_ones, NN, preferred_element_type=F32)
            + lax.dot_general(lo, head_ones, NN, preferred_element_type=F32))


def _wkv(r, k, v, tw, wd2, ta, wa2, w0, a0, k_k, k_a, *, direction, pairs=16):
    B, T, C = r.shape
    L = WKV_CHUNK
    pairs = min(pairs, C // LANES)
    wl = pairs * LANES
    nc = T // L
    R = wd2.shape[0]
    reverse = direction == 1

    def chunk(c):
        return (nc - 1 - c) if reverse else c

    seq = pl.BlockSpec((1, L, wl), lambda b, g, c: (b, chunk(c), g))
    low = pl.BlockSpec((1, L, R), lambda b, g, c: (b, chunk(c), direction))
    fac = pl.BlockSpec((R, wl), lambda b, g, c: (0, g))
    par = pl.BlockSpec((1, wl), lambda b, g, c: (0, g))
    return pl.pallas_call(
        functools.partial(_wkv_body, reverse=reverse, L=L, pairs=pairs),
        out_shape=jax.ShapeDtypeStruct((B, T, C), F32),
        grid=(B, C // wl, nc),
        in_specs=[seq, seq, seq, low, fac, low, fac, par, par, par, par],
        out_specs=seq,
        scratch_shapes=[pltpu.VMEM((pairs, LANES, LANES), F32)],
        compiler_params=_params("parallel", "parallel", "arbitrary"),
        name="wkv7_rev" if reverse else "wkv7_fwd",
    )(r, k, v, tw, wd2, ta, wa2, w0, a0, k_k, k_a)


def _wkv_out_body(of_ref, ob_ref, r_ref, k_ref, v_ref, ta_ref, wa20_ref, wa21_ref, g_ref,
                  a00_ref, a01_ref, ka_ref, rk_ref, lw_ref, lb_ref, o_ref):
    hr = lax.broadcasted_iota(jnp.int32, (LANES, LANES), 0) // RWKV_HEAD_DIM
    hc = lax.broadcasted_iota(jnp.int32, (LANES, LANES), 1) // RWKV_HEAD_DIM
    head_ones = jnp.where(hr == hc, 1.0, 0.0).astype(BF16)
    inv_n = 1.0 / RWKV_HEAD_DIM
    R = wa20_ref.shape[0]
    za0 = jnp.dot(ta_ref[:, :R], wa20_ref[...], preferred_element_type=F32)
    za1 = jnp.dot(ta_ref[:, R:], wa21_ref[...], preferred_element_type=F32)
    asum_all = jax.nn.sigmoid(a00_ref[...] + za0) + jax.nn.sigmoid(a01_ref[...] + za1)
    for t in range(o_ref.shape[1] // LANES):
        sl = slice(t * LANES, (t + 1) * LANES)
        o = of_ref[:, sl] + ob_ref[:, sl]
        mu = _head_sum(o, head_ones) * inv_n
        d = o - mu
        var = _head_sum(d * d, head_ones) * inv_n
        on = d * lax.rsqrt(var + RWKV_GN_EPS) * lw_ref[:, sl] + lb_ref[:, sl]
        asum = asum_all[:, sl]
        ksum = k_ref[:, sl] * (2.0 + (asum - 2.0) * ka_ref[:, sl])
        bonus = _head_sum(r_ref[:, sl] * ksum * rk_ref[:, sl], head_ones) * v_ref[:, sl]
        o_ref[:, sl] = ((on + bonus) * g_ref[:, sl]).astype(o_ref.dtype)


def _wkv_out(o_f, o_b, r, k, v, ta, wa2, g, a0, k_a, r_k, lnx_w, lnx_b):
    M, C = r.shape
    tm = _blk(M, 256, 8)
    tc = _blk(C, 512, LANES)
    R = wa2[0].shape[0]
    seq = pl.BlockSpec((tm, tc), lambda i, j: (i, j))
    low = pl.BlockSpec((tm, 2 * R), lambda i, j: (i, 0))
    fac = pl.BlockSpec((R, tc), lambda i, j: (0, j))
    par = pl.BlockSpec((1, tc), lambda i, j: (0, j))
    row = lambda x: x.reshape(1, C).astype(F32)
    return pl.pallas_call(
        _wkv_out_body,
        out_shape=jax.ShapeDtypeStruct((M, C), BF16),
        grid=(M // tm, C // tc),
        in_specs=[seq] * 5 + [low, fac, fac, seq] + [par] * 6,
        out_specs=seq,
        compiler_params=_params("parallel", "parallel"),
        name="wkv_out",
    )(o_f, o_b, r, k, v, ta, wa2[0], wa2[1], g,
      row(a0[0]), row(a0[1]), row(k_a), row(r_k), row(lnx_w), row(lnx_b))


def _ffn(x2, p):
    h = _rmsnorm(x2, p["norm_ffn"], out_dtype=BF16)
    hidden = p["w_down"].shape[0]
    mid = _swiglu_up(h, p["w_gu"], hidden)
    return _matmul_residual(mid, p["w_down"], x2, tm=512, tn=512, name="ffn_down")


def _rwkv_layer(x, p, v_first):
    B, T, C = x.shape
    M = B * T
    x2 = x.reshape(M, C)
    xr, xw, xk, xv, xa, xg = (t.reshape(M, C) for t in _premix(x, p["norm_mix"], p["mix"]))
    r = _matmul(xr, p["w_r"], out_dtype=F32, name="rwkv_r")
    k = _matmul(xk, p["w_k"], out_dtype=F32, name="rwkv_k")
    v = _matmul(xv, p["w_v"], out_dtype=F32, name="rwkv_v")
    tw = _matmul(xw, p["decay1"], out_dtype=BF16, epilogue=functools.partial(_epi_plain, act="tanh"),
                 name="rwkv_decay1")
    ta = _matmul(xa, p["a1"], out_dtype=BF16, name="rwkv_a1")
    if v_first is not None:
        tv = _matmul(xv, p["v1"], out_dtype=BF16, name="rwkv_v1")
        tm_v, tn_v = _mm_tiles(M, p["v2"].shape[0], C, None, None)
        tile = pl.BlockSpec((tm_v, tn_v), lambda i, j: (i, j))
        v = _matmul(tv, p["v2"], out_dtype=F32, epilogue=_epi_value_residual,
                    extra=(v, v_first, p["v0"].reshape(1, C).astype(F32)),
                    extra_specs=(tile, tile, pl.BlockSpec((1, tn_v), lambda i, j: (0, j))), name="rwkv_v2")
    tg = _matmul(xg, p["g1"], out_dtype=BF16, epilogue=functools.partial(_epi_plain, act="sigmoid"),
                 name="rwkv_g1")
    g = _matmul(tg, p["g2"], out_dtype=F32, name="rwkv_g2")
    seq = lambda t: t.reshape(B, T, C)
    row = lambda t: t.reshape(1, C).astype(F32)
    low = lambda t: t.reshape(B, T, t.shape[1])
    o_dir = [
        _wkv(seq(r), seq(k), seq(v), low(tw), p["decay2"][d], low(ta), p["a2"][d],
             row(p["w_decay0"][d]), row(p["a0"][d]), row(p["k_k"]), row(p["k_a"]), direction=d)
        for d in range(2)
    ]
    y = _wkv_out(o_dir[0].reshape(M, C), o_dir[1].reshape(M, C), r, k, v, ta, p["a2"], g,
                 p["a0"], p["k_a"], p["r_k"], p["lnx_w"], p["lnx_b"])
    x2 = _matmul_residual(y, p["w_o"], x2, name="rwkv_o")
    return x2, v


def _mla_layer(x, p):
    B, T, C = x.shape
    M = B * T
    x2 = x.reshape(M, C)
    H = p["w_uq"].shape[1] // MLA_QK_PAD
    q_rank = p["q_norm"].shape[0]
    kv_rank = p["kv_norm"].shape[0]
    h = _rmsnorm(x2, p["norm_mix"], out_dtype=BF16)
    lat = _matmul(h, p["w_in"], out_dtype=F32, tn=256, name="mla_in")
    c_q = _rmsnorm(lat, p["q_norm"], out_dtype=BF16, width=q_rank, col_block=0)
    c_kv = _rmsnorm(lat, p["kv_norm"], out_dtype=BF16, width=kv_rank, col_block=q_rank // kv_rank)
    tables = _rope_tables(T, MLA_ROPE_DIM)
    half = MLA_ROPE_DIM // 2
    k_rope = _rope_cols(lat, tables, T, (q_rank + kv_rank) // LANES, half)
    tm = _blk(T, 1024, 8)
    qscale = (MLA_NOPE_DIM + MLA_ROPE_DIM) ** -0.5 * math.log2(math.e)
    q = _matmul(c_q, p["w_uq"], out_dtype=BF16, tm=tm, tn=_blk(H * MLA_QK_PAD, 1024, MLA_QK_PAD),
                epilogue=functools.partial(_epi_rope_second_tile, half=half, qscale=qscale),
                extra=tables, extra_specs=_table_specs(T, _blk(M, tm, 8)), name="mla_uq")
    tm_kv = _blk(M, 1024, 8)
    tn_kv = _blk(H * MLA_V_DIM, 2048, MLA_QK_PAD)
    kv = _matmul(c_kv, p["w_ukv"], out_dtype=BF16, tm=tm_kv, tn=tn_kv,
                 epilogue=functools.partial(_epi_add_second_tile, n_add_blocks=H * MLA_QK_PAD // tn_kv),
                 extra=(k_rope,), extra_specs=(pl.BlockSpec((tm_kv, LANES), lambda i, j: (i, 0)),),
                 name="mla_ukv")
    q3 = q.reshape(B, T, H * MLA_QK_PAD)
    kv3 = kv.reshape(B, T, kv.shape[1])
    o = _flash(q3, kv3, kv3[:, :, H * MLA_QK_PAD:], heads=H, n_maps=1, dqk=MLA_QK_PAD, dv=MLA_V_DIM,
               q_blk0=0, k_blk0=0)
    return _matmul_residual(o.reshape(M, H * MLA_V_DIM), p["w_o"], x2, name="mla_o")


def _diff_layer(x, p, lambda_init):
    B, T, C = x.shape
    M = B * T
    x2 = x.reshape(M, C)
    H = C // (2 * DIFF_HEAD_DIM)
    h = _rmsnorm(x2, p["norm_mix"], out_dtype=BF16)
    tables = _rope_tables(T, DIFF_ROT_DIM)
    tm = _blk(T, 1024, 8)
    tn = _blk(C, 512, LANES)
    qkv = _matmul(h, p["w_qkv"], out_dtype=BF16, tm=tm, tn=tn,
                  epilogue=functools.partial(_epi_rope_tiles, half=DIFF_ROT_DIM // 2,
                                             n_rope_blocks=2 * C // tn, n_q_blocks=C // tn,
                                             qscale=DIFF_HEAD_DIM ** -0.5 * math.log2(math.e)),
                  extra=tables, extra_specs=_table_specs(T, _blk(M, tm, 8)), name="diff_qkv")
    qkv3 = qkv.reshape(B, T, 3 * C)
    o = _flash(qkv3, qkv3, qkv3[:, :, 2 * C:], heads=H, n_maps=2, dqk=DIFF_HEAD_DIM,
               dv=2 * DIFF_HEAD_DIM, q_blk0=0, k_blk0=H,
               diff_params=(p["lambda"].astype(F32), p["subln"].reshape(1, -1).astype(F32)),
               lambda_init=lambda_init)
    return _matmul_residual(o.reshape(M, C), p["w_o"], x2, name="diff_o")


def _trunk(x, layers, final_norm):
    B, T, C = x.shape
    v_first = None
    for i, p in enumerate(layers):
        kind = p["kind"]
        if kind == "rwkv":
            x2, v_layer = _rwkv_layer(x, p, v_first)
            if v_first is None:
                v_first = v_layer
        elif kind == "mla":
            x2 = _mla_layer(x, p)
        else:
            x2 = _diff_layer(x, p, 0.8 - 0.6 * math.exp(-0.3 * i))
        x2 = _ffn(x2, p)
        x = x2.reshape(B, T, C)
    return _rmsnorm(x.reshape(B * T, C), final_norm).reshape(B, T, C)


def _prep_ffn(p):
    return {"norm_ffn": p["norm_ffn"], "w_gu": p["w_gu"].astype(BF16), "w_down": p["w_down"].astype(BF16)}


def _prep_rwkv(p):
    b = lambda w: w.astype(BF16)
    out = dict(kind="rwkv", norm_mix=p["norm_mix"], mix=p["mix"],
               w_r=b(p["w_rkv"][0]), w_k=b(p["w_rkv"][1]), w_v=b(p["w_rkv"][2]),
               w_decay0=p["w_decay0"], a0=p["a0"], k_k=p["k_k"], k_a=p["k_a"], r_k=p["r_k"],
               lnx_w=p["lnx_w"], lnx_b=p["lnx_b"], w_o=b(p["w_o"]))
    for name1, name2, key1, key2 in (("w_decay1", "w_decay2", "decay1", "decay2"), ("a1", "a2", "a1", "a2")):
        w1 = [_pad_to(p[name1][d], 1, LANES) for d in range(2)]
        out[key1] = b(jnp.concatenate(w1, axis=1))
        out[key2] = [b(_pad_to(p[name2][d], 0, LANES)) for d in range(2)]
    if "v0" in p:
        out["v0"] = p["v0"]
        out["v1"] = b(_pad_to(p["v1"], 1, LANES))
        out["v2"] = b(_pad_to(p["v2"], 0, LANES))
    out["g1"] = b(_pad_to(p["g1"], 1, LANES))
    out["g2"] = b(_pad_to(p["g2"], 0, LANES))
    out.update(_prep_ffn(p))
    return out


def _prep_mla(p):
    b = lambda w: w.astype(BF16)
    q_rank = p["q_norm"].shape[0]
    kv_rank = p["kv_norm"].shape[0]
    qk = MLA_NOPE_DIM + MLA_ROPE_DIM
    H = p["w_uq"].shape[1] // qk
    w_uq = _pad_to(p["w_uq"].reshape(q_rank, H, qk), 2, MLA_QK_PAD).reshape(q_rank, H * MLA_QK_PAD)
    w_ukv = p["w_ukv"].reshape(kv_rank, H, MLA_NOPE_DIM + MLA_V_DIM)
    w_k = _pad_to(w_ukv[:, :, :MLA_NOPE_DIM], 2, MLA_QK_PAD).reshape(kv_rank, H * MLA_QK_PAD)
    w_v = w_ukv[:, :, MLA_NOPE_DIM:].reshape(kv_rank, H * MLA_V_DIM)
    out = dict(kind="mla", norm_mix=p["norm_mix"], w_in=b(_pad_to(p["w_in"], 1, 256)),
               q_norm=p["q_norm"], kv_norm=p["kv_norm"], w_uq=b(w_uq),
               w_ukv=b(jnp.concatenate([w_k, w_v], axis=1)), w_o=b(p["w_o"]))
    out.update(_prep_ffn(p))
    return out


def _prep_diff(p):
    out = dict(kind="diff", norm_mix=p["norm_mix"], w_qkv=p["w_qkv"].astype(BF16), subln=p["subln"],
               w_o=p["w_o"].astype(BF16))
    out["lambda"] = p["lambda"]
    out.update(_prep_ffn(p))
    return out


def _forward(x_prompt, x_sample, raw_layers, final_norm):
    prep = {"rwkv": _prep_rwkv, "mla": _prep_mla, "diff": _prep_diff}
    kinds = ("rwkv", "mla", "diff")
    layers = [prep[kinds[i % 3]](p) for i, p in enumerate(raw_layers)]
    return (_trunk(x_prompt, layers, final_norm), _trunk(x_sample, layers, final_norm))


def kernel(x_prompt, x_sample, l0_norm_mix, l0_mix, l0_w_rkv, l0_w_decay0, l0_w_decay1, l0_w_decay2, l0_a0, l0_a1, l0_a2, l0_g1, l0_g2, l0_k_k, l0_k_a, l0_r_k, l0_lnx_w, l0_lnx_b, l0_w_o, l0_norm_ffn, l0_w_gu, l0_w_down, l1_norm_mix, l1_w_in, l1_q_norm, l1_kv_norm, l1_w_uq, l1_w_ukv, l1_w_o, l1_norm_ffn, l1_w_gu, l1_w_down, l2_norm_mix, l2_w_qkv, l2_lambda, l2_subln, l2_w_o, l2_norm_ffn, l2_w_gu, l2_w_down, l3_norm_mix, l3_mix, l3_w_rkv, l3_w_decay0, l3_w_decay1, l3_w_decay2, l3_a0, l3_a1, l3_a2, l3_v0, l3_v1, l3_v2, l3_g1, l3_g2, l3_k_k, l3_k_a, l3_r_k, l3_lnx_w, l3_lnx_b, l3_w_o, l3_norm_ffn, l3_w_gu, l3_w_down, final_norm):
    raw_layers = (
        dict(norm_mix=l0_norm_mix, mix=l0_mix, w_rkv=l0_w_rkv, w_decay0=l0_w_decay0, w_decay1=l0_w_decay1,
             w_decay2=l0_w_decay2, a0=l0_a0, a1=l0_a1, a2=l0_a2, g1=l0_g1, g2=l0_g2, k_k=l0_k_k, k_a=l0_k_a,
             r_k=l0_r_k, lnx_w=l0_lnx_w, lnx_b=l0_lnx_b, w_o=l0_w_o,
             norm_ffn=l0_norm_ffn, w_gu=l0_w_gu, w_down=l0_w_down),
        dict(norm_mix=l1_norm_mix, w_in=l1_w_in, q_norm=l1_q_norm, kv_norm=l1_kv_norm, w_uq=l1_w_uq,
             w_ukv=l1_w_ukv, w_o=l1_w_o, norm_ffn=l1_norm_ffn, w_gu=l1_w_gu, w_down=l1_w_down),
        dict(norm_mix=l2_norm_mix, w_qkv=l2_w_qkv, subln=l2_subln, w_o=l2_w_o,
             norm_ffn=l2_norm_ffn, w_gu=l2_w_gu, w_down=l2_w_down, **{"lambda": l2_lambda}),
        dict(norm_mix=l3_norm_mix, mix=l3_mix, w_rkv=l3_w_rkv, w_decay0=l3_w_decay0, w_decay1=l3_w_decay1,
             w_decay2=l3_w_decay2, a0=l3_a0, a1=l3_a1, a2=l3_a2, v0=l3_v0, v1=l3_v1, v2=l3_v2,
             g1=l3_g1, g2=l3_g2, k_k=l3_k_k, k_a=l3_k_a, r_k=l3_r_k, lnx_w=l3_lnx_w, lnx_b=l3_lnx_b, w_o=l3_w_o,
             norm_ffn=l3_norm_ffn, w_gu=l3_w_gu, w_down=l3_w_down),
    )
    return _forward(x_prompt, x_sample, raw_layers, final_norm)
```

```python
import functools
import math

import jax
import jax.numpy as jnp
from jax import lax
from jax.experimental import pallas as pl
from jax.experimental.pallas import tpu as pltpu

F32 = jnp.float32
BF16 = jnp.bfloat16

LANES = 128
VMEM_LIMIT = 56 * 1024 * 1024

RMS_EPS = 1e-6
ROPE_THETA = 500000.0
RWKV_HEAD_DIM = 64
RWKV_GN_EPS = 64e-5
WKV_CHUNK = 64
MLA_NOPE_DIM = 128
MLA_ROPE_DIM = 64
MLA_V_DIM = 128
MLA_QK_PAD = 256
DIFF_HEAD_DIM = 128
DIFF_ROT_DIM = DIFF_HEAD_DIM // 4
DIFF_EPS = 1e-5
DEPTH = 4

NN = (((1,), (0,)), ((), ()))
NT = (((1,), (1,)), ((), ()))
TN = (((0,), (0,)), ((), ()))


def _blk(n, pref, align):
    if n <= pref:
        return n
    d = (pref // align) * align
    while d >= align:
        if n % d == 0:
            return d
        d -= align
    raise ValueError(f"no block for {n} (pref {pref}, align {align})")


def _params(*sem):
    return pltpu.CompilerParams(dimension_semantics=sem, vmem_limit_bytes=VMEM_LIMIT)


def _pad_to(x, axis, mult):
    n = x.shape[axis]
    pad = (-n) % mult
    if pad == 0:
        return x
    widths = [(0, 0)] * x.ndim
    widths[axis] = (0, pad)
    return jnp.pad(x, widths)


def _rmsnorm_body(x_ref, g_ref, o_ref, *, eps):
    x = x_ref[...].astype(F32)
    y = x * lax.rsqrt(jnp.mean(x * x, axis=-1, keepdims=True) + eps)
    o_ref[...] = (y * g_ref[...]).astype(o_ref.dtype)


def _rmsnorm(x, g, *, eps=RMS_EPS, out_dtype=F32, width=None, col_block=0):
    M = x.shape[0]
    width = x.shape[1] if width is None else width
    tm = _blk(M, 512, 8)
    return pl.pallas_call(
        functools.partial(_rmsnorm_body, eps=eps),
        out_shape=jax.ShapeDtypeStruct((M, width), out_dtype),
        grid=(M // tm,),
        in_specs=[pl.BlockSpec((tm, width), lambda i: (i, col_block)),
                  pl.BlockSpec((1, width), lambda i: (0, 0))],
        out_specs=pl.BlockSpec((tm, width), lambda i: (i, 0)),
        compiler_params=_params("parallel"),
        name="rmsnorm",
    )(x, g.reshape(1, width).astype(F32))


def _premix_body(x_ref, prev_ref, next_ref, g_ref, mix_ref, *o_refs, eps):
    i = pl.program_id(1)
    g = g_ref[...]

    def norm(v):
        return v * lax.rsqrt(jnp.mean(v * v, axis=-1, keepdims=True) + eps) * g

    h = norm(x_ref[0])
    tt = h.shape[0]
    h_before = jnp.where(i == 0, 0.0, norm(prev_ref[0])[7:8])
    h_after = jnp.where(i == pl.num_programs(1) - 1, 0.0, norm(next_ref[0])[0:1])
    row = lax.broadcasted_iota(jnp.int32, (tt, 1), 0)
    h_dn = jnp.where(row == 0, h_before, pltpu.roll(h, 1, 0))
    h_up = jnp.where(row == tt - 1, h_after, pltpu.roll(h, tt - 1, 0))
    dx = 0.5 * (h_dn + h_up) - h
    for n, o_ref in enumerate(o_refs):
        o_ref[0] = (h + dx * mix_ref[n:n + 1, :]).astype(o_ref.dtype)


def _premix(x, g, mix, *, tt=256):
    B, T, C = x.shape
    tt = _blk(T, tt, 8)
    nb = tt // 8
    last = T // 8 - 1
    n_mix = mix.shape[0]
    out = jax.ShapeDtypeStruct((B, T, C), BF16)
    tile = pl.BlockSpec((1, tt, C), lambda b, i: (b, i, 0))
    return pl.pallas_call(
        functools.partial(_premix_body, eps=RMS_EPS),
        out_shape=[out] * n_mix,
        grid=(B, T // tt),
        in_specs=[tile,
                  pl.BlockSpec((1, 8, C), lambda b, i: (b, jnp.maximum(i * nb - 1, 0), 0)),
                  pl.BlockSpec((1, 8, C), lambda b, i: (b, jnp.minimum((i + 1) * nb, last), 0)),
                  pl.BlockSpec((1, C), lambda b, i: (0, 0)),
                  pl.BlockSpec((n_mix, C), lambda b, i: (0, 0))],
        out_specs=[tile] * n_mix,
        compiler_params=_params("parallel", "parallel"),
        name="rwkv_premix",
    )(x, x, x, g.reshape(1, C).astype(F32), mix.astype(F32))


def _rope_tables(T, dim):
    half = dim // 2
    inv_freq = 1.0 / (ROPE_THETA ** (jnp.arange(0, dim, 2, dtype=F32) / dim))
    ang = jnp.arange(T, dtype=F32)[:, None] * inv_freq[None, :]
    c, s = jnp.cos(ang), jnp.sin(ang)
    ones = jnp.ones((T, LANES - dim), F32)
    zeros = jnp.zeros((T, LANES - dim), F32)
    zh = jnp.zeros((T, half), F32)
    C = jnp.concatenate([c, c, ones], axis=1)
    S1 = jnp.concatenate([-s, zh, zeros], axis=1)
    S2 = jnp.concatenate([zh, s, zeros], axis=1)
    return C, S1, S2


def _rope_tile(t, c, s1, s2, half):
    return t * c + pltpu.roll(t, LANES - half, 1) * s1 + pltpu.roll(t, half, 1) * s2


def _rope_cols_body(x_ref, c_ref, s1_ref, s2_ref, o_ref, *, half):
    o_ref[...] = _rope_tile(x_ref[...].astype(F32), c_ref[...], s1_ref[...], s2_ref[...], half)


def _rope_cols(x, tables, T, col_block, half):
    M = x.shape[0]
    tm = _blk(T, 512, 8)
    nt = T // tm
    tab = pl.BlockSpec((tm, LANES), lambda i: (i % nt, 0))
    return pl.pallas_call(
        functools.partial(_rope_cols_body, half=half),
        out_shape=jax.ShapeDtypeStruct((M, LANES), F32),
        grid=(M // tm,),
        in_specs=[pl.BlockSpec((tm, LANES), lambda i: (i, col_block)), tab, tab, tab],
        out_specs=pl.BlockSpec((tm, LANES), lambda i: (i, 0)),
        compiler_params=_params("parallel"),
        name="rope_cols",
    )(x, *tables)


def _mm_body(*refs, n_extra, epilogue):
    x_ref, w_ref = refs[0], refs[1]
    extra = refs[2:2 + n_extra]
    o_ref = refs[2 + n_extra]
    acc = jnp.dot(x_ref[...], w_ref[...], preferred_element_type=F32)
    epilogue(acc, extra, o_ref)


def _epi_plain(acc, extra, o_ref, *, act):
    if act == "tanh":
        acc = jnp.tanh(acc)
    elif act == "sigmoid":
        acc = jax.nn.sigmoid(acc)
    o_ref[...] = acc.astype(o_ref.dtype)


def _epi_residual(acc, extra, o_ref):
    o_ref[...] = (extra[0][...] + acc).astype(o_ref.dtype)


def _epi_value_residual(acc, extra, o_ref):
    v = extra[0][...]
    o_ref[...] = (v + (extra[1][...] - v) * jax.nn.sigmoid(extra[2][...] + acc)).astype(o_ref.dtype)


def _epi_rope_second_tile(acc, extra, o_ref, *, half, qscale):
    c, s1, s2 = (e[...] for e in extra)
    acc = acc * qscale
    for t in range(acc.shape[1] // LANES):
        sl = slice(t * LANES, (t + 1) * LANES)
        tile = _rope_tile(acc[:, sl], c, s1, s2, half) if t % 2 else acc[:, sl]
        o_ref[:, sl] = tile.astype(o_ref.dtype)


def _epi_rope_tiles(acc, extra, o_ref, *, half, n_rope_blocks, n_q_blocks, qscale):
    j = pl.program_id(1)

    @pl.when(j < n_rope_blocks)
    def _():
        c, s1, s2 = (e[...] for e in extra)
        scaled = acc * jnp.where(j < n_q_blocks, qscale, 1.0).astype(F32)
        for t in range(acc.shape[1] // LANES):
            sl = slice(t * LANES, (t + 1) * LANES)
            o_ref[:, sl] = _rope_tile(scaled[:, sl], c, s1, s2, half).astype(o_ref.dtype)

    @pl.when(j >= n_rope_blocks)
    def _():
        o_ref[...] = acc.astype(o_ref.dtype)


def _epi_add_second_tile(acc, extra, o_ref, *, n_add_blocks):
    j = pl.program_id(1)

    @pl.when(j < n_add_blocks)
    def _():
        shared = extra[0][...]
        for t in range(acc.shape[1] // LANES):
            sl = slice(t * LANES, (t + 1) * LANES)
            tile = acc[:, sl] + shared if t % 2 else acc[:, sl]
            o_ref[:, sl] = tile.astype(o_ref.dtype)

    @pl.when(j >= n_add_blocks)
    def _():
        o_ref[...] = acc.astype(o_ref.dtype)


X_TILE_BYTES = 8 * 1024 * 1024


def _mm_tiles(M, K, N, tm, tn):
    if tm is None:
        tm = 1024 if 1024 * K * 2 <= X_TILE_BYTES else 512
    if tn is None:
        tn = 512
    return _blk(M, tm, 8), _blk(N, tn, LANES)


def _matmul(x, w, *, out_dtype, epilogue=None, extra=(), extra_specs=(), tm=None, tn=None,
            name="matmul"):
    M = x.shape[0]
    K, N = w.shape
    tm, tn = _mm_tiles(M, K, N, tm, tn)
    if epilogue is None:
        epilogue = functools.partial(_epi_plain, act=None)
    return pl.pallas_call(
        functools.partial(_mm_body, n_extra=len(extra), epilogue=epilogue),
        out_shape=jax.ShapeDtypeStruct((M, N), out_dtype),
        grid=(M // tm, N // tn),
        in_specs=[pl.BlockSpec((tm, K), lambda i, j: (i, 0)),
                  pl.BlockSpec((K, tn), lambda i, j: (0, j))] + list(extra_specs),
        out_specs=pl.BlockSpec((tm, tn), lambda i, j: (i, j)),
        compiler_params=_params("parallel", "arbitrary"),
        name=name,
    )(x, w, *extra)


def _matmul_residual(x, w, res, *, tm=None, tn=None, name="matmul_residual"):
    M = x.shape[0]
    K, N = w.shape
    tm_, tn_ = _mm_tiles(M, K, N, tm, tn)
    return _matmul(x, w, out_dtype=F32, epilogue=_epi_residual, extra=(res,),
                   extra_specs=(pl.BlockSpec((tm_, tn_), lambda i, j: (i, j)),), tm=tm, tn=tn, name=name)


def _table_specs(T, tm):
    nt = T // tm
    return tuple(pl.BlockSpec((tm, LANES), lambda i, j: (i % nt, 0)) for _ in range(3))


def _swiglu_body(x_ref, wg_ref, wu_ref, o_ref):
    x = x_ref[...]
    g = jnp.dot(x, wg_ref[...], preferred_element_type=F32)
    u = jnp.dot(x, wu_ref[...], preferred_element_type=F32)
    o_ref[...] = (g * jax.nn.sigmoid(g) * u).astype(o_ref.dtype)


def _swiglu_up(x, w_gu, hidden, *, tm=2048, tn=256):
    M, K = x.shape
    tm = _blk(M, tm, 8)
    tn = _blk(hidden, tn, LANES)
    nb = hidden // tn
    return pl.pallas_call(
        _swiglu_body,
        out_shape=jax.ShapeDtypeStruct((M, hidden), BF16),
        grid=(M // tm, nb),
        in_specs=[pl.BlockSpec((tm, K), lambda i, j: (i, 0), pipeline_mode=pl.Buffered(1)),
                  pl.BlockSpec((K, tn), lambda i, j: (0, j)),
                  pl.BlockSpec((K, tn), lambda i, j: (0, j + nb))],
        out_specs=pl.BlockSpec((tm, tn), lambda i, j: (i, j)),
        compiler_params=_params("parallel", "arbitrary"),
        name="swiglu_up",
    )(x, w_gu, w_gu)


def _flash_body(*refs, n_maps, dqk, dv, diff, lambda_init, bks):
    if diff:
        q_ref, k_ref, vt_ref, lam_ref, g_ref, o_ref, m_scr, acc_scr = refs
    else:
        q_ref, k_ref, vt_ref, o_ref, m_scr, acc_scr = refs
    j = pl.program_id(3)

    @pl.when(j == 0)
    def _():
        m_scr[...] = jnp.full(m_scr.shape, -jnp.inf, F32)
        acc_scr[...] = jnp.zeros(acc_scr.shape, F32)

    q = q_ref[0]
    maps = [slice(mp * dqk, (mp + 1) * dqk) for mp in range(n_maps)]
    m = [m_scr[mp] for mp in range(n_maps)]
    acc = [acc_scr[mp] for mp in range(n_maps)]
    n_sub = k_ref.shape[1] // bks
    st = []
    for sub in range(n_sub):
        k = k_ref[0, sub * bks:(sub + 1) * bks, :]
        st.append([lax.dot_general(k[:, sl], q[:, sl], NT, preferred_element_type=F32) for sl in maps])
    for sub in range(n_sub):
        vt = vt_ref[0, :, sub * bks:(sub + 1) * bks]
        for mp in range(n_maps):
            s_t = st[sub][mp]
            m_new = jnp.maximum(m[mp], jnp.max(s_t, axis=0, keepdims=True))
            alpha = jnp.exp2(m[mp] - m_new)
            p = jnp.exp2(s_t - m_new).astype(vt.dtype)
            acc[mp] = alpha * acc[mp] + jnp.dot(vt, p, preferred_element_type=F32)
            m[mp] = m_new
    for mp in range(n_maps):
        m_scr[mp] = m[mp]
        acc_scr[mp] = acc[mp]

    @pl.when(j == pl.num_programs(3) - 1)
    def _():
        o = [(a[:dv] / a[dv:dv + 1]).T for a in acc]
        if diff:
            lp = lam_ref[...]
            lam = (jnp.exp(jnp.sum(lp[0:1] * lp[1:2], axis=-1, keepdims=True))
                   - jnp.exp(jnp.sum(lp[2:3] * lp[3:4], axis=-1, keepdims=True)) + lambda_init)
            od = o[0] - lam * o[1]
            y = od * lax.rsqrt(jnp.mean(od * od, axis=-1, keepdims=True) + DIFF_EPS) * g_ref[...]
            o_ref[0] = (y * (1.0 - lambda_init)).astype(o_ref.dtype)
        else:
            o_ref[0] = o[0].astype(o_ref.dtype)


ONES_ROWS = 16


def _v_transposed(v, heads, dv):
    B, T = v.shape[0], v.shape[1]
    vt = jnp.swapaxes(v, 1, 2).reshape(B, heads, dv, T)
    ones = jnp.ones((B, heads, ONES_ROWS, T), v.dtype)
    return jnp.concatenate([vt, ones], axis=2).reshape(B, heads * (dv + ONES_ROWS), T)


def _flash(q_arr, k_arr, v_arr, *, heads, n_maps, dqk, dv, q_blk0, k_blk0,
           diff_params=None, lambda_init=0.0, bq=512, bk=8192, bks=256):
    B, T = q_arr.shape[0], q_arr.shape[1]
    bq = _blk(T, bq, LANES)
    bk = _blk(T, bk, LANES)
    bks = _blk(bk, bks, LANES)
    wqk = n_maps * dqk
    dve = dv + ONES_ROWS
    diff = diff_params is not None
    in_specs = [pl.BlockSpec((1, bq, wqk), lambda b, h, i, j: (b, i, q_blk0 + h)),
                pl.BlockSpec((1, bk, wqk), lambda b, h, i, j: (b, j, k_blk0 + h)),
                pl.BlockSpec((1, dve, bk), lambda b, h, i, j: (b, h, j))]
    args = [q_arr, k_arr, _v_transposed(v_arr, heads, dv)]
    if diff:
        lam, subln = diff_params
        in_specs += [pl.BlockSpec(lam.shape, lambda b, h, i, j: (0, 0)),
                     pl.BlockSpec(subln.shape, lambda b, h, i, j: (0, 0))]
        args += [lam, subln]
    return pl.pallas_call(
        functools.partial(_flash_body, n_maps=n_maps, dqk=dqk, dv=dv, diff=diff,
                          lambda_init=lambda_init, bks=bks),
        out_shape=jax.ShapeDtypeStruct((B, T, heads * dv), BF16),
        grid=(B, heads, T // bq, T // bk),
        in_specs=in_specs,
        out_specs=pl.BlockSpec((1, bq, dv), lambda b, h, i, j: (b, i, h)),
        scratch_shapes=[pltpu.VMEM((n_maps, 1, bq), F32),
                        pltpu.VMEM((n_maps, dve, bq), F32)],
        compiler_params=_params("parallel", "parallel", "parallel", "arbitrary"),
        name="flash_diff" if diff else "flash_mla",
    )(*args)


def _split_dot(m_exact, x, n_parts):
    acc = None
    rem = x
    for _ in range(n_parts):
        part = rem.astype(BF16)
        rem = rem - part.astype(F32)
        d = lax.dot_general(m_exact, part, NN, preferred_element_type=F32)
        acc = d if acc is None else acc + d
    return acc


def _wkv_body(r_ref, k_ref, v_ref, tw_ref, wd2_ref, ta_ref, wa2_ref, w0_ref, a0_ref, kk_ref, ka_ref,
              o_ref, s_scr,
              *, reverse, L, pairs):
    c = pl.program_id(2)

    @pl.when(c == 0)
    def _():
        s_scr[...] = jnp.zeros(s_scr.shape, F32)

    L2 = 2 * L
    ri = lax.broadcasted_iota(jnp.int32, (L, L), 0)
    ci = lax.broadcasted_iota(jnp.int32, (L, L), 1)
    tri = ((ri <= ci) if reverse else (ri >= ci)).astype(BF16)
    r2 = lax.broadcasted_iota(jnp.int32, (L2, L2), 0)
    c2 = lax.broadcasted_iota(jnp.int32, (L2, L2), 1)
    strict = (r2 < c2) if reverse else (r2 > c2)
    incl = (r2 <= c2) if reverse else (r2 >= c2)
    eye = jnp.where(r2 == c2, 1.0, 0.0).astype(F32)
    lane = lax.broadcasted_iota(jnp.int32, (L, LANES), 1)
    head0 = lane < RWKV_HEAD_DIM
    hr = lax.broadcasted_iota(jnp.int32, (LANES, LANES), 0) // RWKV_HEAD_DIM
    hc = lax.broadcasted_iota(jnp.int32, (LANES, LANES), 1) // RWKV_HEAD_DIM
    head_ones = jnp.where(hr == hc, 1.0, 0.0).astype(BF16)

    def stack(q):
        return jnp.concatenate([jnp.where(head0, q, 0.0), jnp.where(head0, 0.0, q)], axis=0).astype(BF16)

    dotf = lambda a, b, dims: lax.dot_general(a, b, dims, preferred_element_type=F32)
    P = range(pairs)
    lanes = [slice(p * LANES, (p + 1) * LANES) for p in P]

    zw = jnp.dot(tw_ref[0], wd2_ref[...], preferred_element_type=F32)
    za = jnp.dot(ta_ref[0], wa2_ref[...], preferred_element_type=F32)
    lw = -math.exp(-0.5) * jax.nn.sigmoid(w0_ref[...] + zw)
    asig = jax.nn.sigmoid(a0_ref[...] + za)
    k_all = k_ref[0]
    kkr = k_all * kk_ref[...]
    kkr2 = kkr * kkr
    n2 = jnp.concatenate([_head_sum(kkr2[:, sl], head_ones) for sl in lanes], axis=1)
    kk = kkr / jnp.maximum(jnp.sqrt(n2), 1e-12)
    kd = k_all * (1.0 + (asig - 1.0) * ka_ref[...])
    b_all = kk * asig
    cum = _split_dot(tri, lw, 3)
    tot = jnp.sum(lw, axis=0, keepdims=True)
    a_t = -kk * jnp.exp(cum - lw)
    r_t = r_ref[0] * jnp.exp(cum)
    pinv = jnp.exp(-cum)
    pend = jnp.exp(tot - cum)
    decay = jnp.exp(tot)
    b_in, k_in = b_all * pinv, kd * pinv
    b_end, k_end = b_all * pend, kd * pend

    As = [stack(a_t[:, sl]) for sl in lanes]
    Bs = [stack(b_in[:, sl]) for sl in lanes]
    m_ab = [jnp.where(strict, dotf(As[p], Bs[p], NT), 0.0) for p in P]

    inv = None
    s = 1
    while s < L:
        sh = s.bit_length() - 1
        off_diag = ((r2 >> (sh + 1)) == (c2 >> (sh + 1))) & ((r2 >> sh) != (c2 >> sh))
        e = [jnp.where(off_diag, m_ab[p], 0.0) for p in P]
        if s == 1:
            inv = [eye + e[p] for p in P]
        else:
            invb = [inv[p].astype(BF16) for p in P]
            t = [dotf(e[p].astype(BF16), invb[p], NN).astype(BF16) for p in P]
            inv = [inv[p] + dotf(invb[p], t[p], NN) for p in P]
        s *= 2

    Rs = [stack(r_t[:, sl]) for sl in lanes]
    Ks = [stack(k_in[:, sl]) for sl in lanes]
    Vs = [stack(v_ref[0, :, sl]) for sl in lanes]
    m_ak = [jnp.where(strict, dotf(As[p], Ks[p], NT), 0.0).astype(BF16) for p in P]
    m_r = [jnp.concatenate([jnp.where(incl, dotf(Rs[p], Bs[p], NT), 0.0),
                            jnp.where(incl, dotf(Rs[p], Ks[p], NT), 0.0)], axis=1).astype(BF16) for p in P]
    state = [s_scr[p] for p in P]
    sb = [state[p].astype(BF16) for p in P]
    w_s = [dotf(As[p], sb[p], NT) + dotf(m_ak[p], Vs[p], NN) for p in P]
    u_s = [dotf(inv[p].astype(BF16), w_s[p].astype(BF16), NN).astype(BF16) for p in P]
    uv = [jnp.concatenate([u_s[p], Vs[p]], axis=0) for p in P]
    o_s = [dotf(Rs[p], sb[p], NT) + dotf(m_r[p], uv[p], NN) for p in P]
    bk_end = [jnp.concatenate([stack(b_end[:, sl]), stack(k_end[:, sl])], axis=0) for sl in lanes]
    new_state = [state[p] * decay[:, lanes[p]] + dotf(uv[p], bk_end[p], TN) for p in P]
    o_ref[0] = jnp.concatenate([o[:L] + o[L:] for o in o_s], axis=1)
    s_scr[...] = jnp.stack(new_state)


def _head_sum(x, head_ones):
    hi = x.astype(BF16)
    lo = (x - hi.astype(F32)).astype(BF16)
    return (lax.dot_general(hi, head_ones, NN, preferred_element_type=F32)
            + lax.dot_general(lo, head_ones, NN, preferred_element_type=F32))


def _wkv(r, k, v, tw, wd2, ta, wa2, w0, a0, k_k, k_a, *, direction, pairs=32):
    B, T, C = r.shape
    L = WKV_CHUNK
    pairs = min(pairs, C // LANES)
    wl = pairs * LANES
    nc = T // L
    R = wd2.shape[0]
    reverse = direction == 1

    def chunk(c):
        return (nc - 1 - c) if reverse else c

    seq = pl.BlockSpec((1, L, wl), lambda b, g, c: (b, chunk(c), g))
    low = pl.BlockSpec((1, L, R), lambda b, g, c: (b, chunk(c), direction))
    fac = pl.BlockSpec((R, wl), lambda b, g, c: (0, g))
    par = pl.BlockSpec((1, wl), lambda b, g, c: (0, g))
    return pl.pallas_call(
        functools.partial(_wkv_body, reverse=reverse, L=L, pairs=pairs),
        out_shape=jax.ShapeDtypeStruct((B, T, C), F32),
        grid=(B, C // wl, nc),
        in_specs=[seq, seq, seq, low, fac, low, fac, par, par, par, par],
        out_specs=seq,
        scratch_shapes=[pltpu.VMEM((pairs, LANES, LANES), F32)],
        compiler_params=_params("parallel", "parallel", "arbitrary"),
        name="wkv7_rev" if reverse else "wkv7_fwd",
    )(r, k, v, tw, wd2, ta, wa2, w0, a0, k_k, k_a)


def _wkv_out_body(of_ref, ob_ref, r_ref, k_ref, v_ref, ta_ref, wa20_ref, wa21_ref, g_ref,
                  a00_ref, a01_ref, ka_ref, rk_ref, lw_ref, lb_ref, o_ref):
    hr = lax.broadcasted_iota(jnp.int32, (LANES, LANES), 0) // RWKV_HEAD_DIM
    hc = lax.broadcasted_iota(jnp.int32, (LANES, LANES), 1) // RWKV_HEAD_DIM
    head_ones = jnp.where(hr == hc, 1.0, 0.0).astype(BF16)
    inv_n = 1.0 / RWKV_HEAD_DIM
    R = wa20_ref.shape[0]
    za0 = jnp.dot(ta_ref[:, :R], wa20_ref[...], preferred_element_type=F32)
    za1 = jnp.dot(ta_ref[:, R:], wa21_ref[...], preferred_element_type=F32)
    asum_all = jax.nn.sigmoid(a00_ref[...] + za0) + jax.nn.sigmoid(a01_ref[...] + za1)
    for t in range(o_ref.shape[1] // LANES):
        sl = slice(t * LANES, (t + 1) * LANES)
        o = of_ref[:, sl] + ob_ref[:, sl]
        mu = _head_sum(o, head_ones) * inv_n
        d = o - mu
        var = _head_sum(d * d, head_ones) * inv_n
        on = d * lax.rsqrt(var + RWKV_GN_EPS) * lw_ref[:, sl] + lb_ref[:, sl]
        asum = asum_all[:, sl]
        ksum = k_ref[:, sl] * (2.0 + (asum - 2.0) * ka_ref[:, sl])
        bonus = _head_sum(r_ref[:, sl] * ksum * rk_ref[:, sl], head_ones) * v_ref[:, sl]
        o_ref[:, sl] = ((on + bonus) * g_ref[:, sl]).astype(o_ref.dtype)


def _wkv_out(o_f, o_b, r, k, v, ta, wa2, g, a0, k_a, r_k, lnx_w, lnx_b):
    M, C = r.shape
    tm = _blk(M, 512, 8)
    tc = _blk(C, 512, LANES)
    R = wa2[0].shape[0]
    seq = pl.BlockSpec((tm, tc), lambda i, j: (i, j))
    low = pl.BlockSpec((tm, 2 * R), lambda i, j: (i, 0))
    fac = pl.BlockSpec((R, tc), lambda i, j: (0, j))
    par = pl.BlockSpec((1, tc), lambda i, j: (0, j))
    row = lambda x: x.reshape(1, C).astype(F32)
    return pl.pallas_call(
        _wkv_out_body,
        out_shape=jax.ShapeDtypeStruct((M, C), BF16),
        grid=(M // tm, C // tc),
        in_specs=[seq] * 5 + [low, fac, fac, seq] + [par] * 6,
        out_specs=seq,
        compiler_params=_params("parallel", "parallel"),
        name="wkv_out",
    )(o_f, o_b, r, k, v, ta, wa2[0], wa2[1], g,
      row(a0[0]), row(a0[1]), row(k_a), row(r_k), row(lnx_w), row(lnx_b))


def _ffn(x2, p):
    h = _rmsnorm(x2, p["norm_ffn"], out_dtype=BF16)
    hidden = p["w_down"].shape[0]
    mid = _swiglu_up(h, p["w_gu"], hidden)
    return _matmul_residual(mid, p["w_down"], x2, tm=512, tn=256, name="ffn_down")


def _rwkv_layer(x, p, v_first):
    B, T, C = x.shape
    M = B * T
    x2 = x.reshape(M, C)
    xr, xw, xk, xv, xa, xg = (t.reshape(M, C) for t in _premix(x, p["norm_mix"], p["mix"]))
    r = _matmul(xr, p["w_r"], out_dtype=F32, name="rwkv_r")
    k = _matmul(xk, p["w_k"], out_dtype=F32, name="rwkv_k")
    v = _matmul(xv, p["w_v"], out_dtype=F32, name="rwkv_v")
    tw = _matmul(xw, p["decay1"], out_dtype=BF16, epilogue=functools.partial(_epi_plain, act="tanh"),
                 name="rwkv_decay1")
    ta = _matmul(xa, p["a1"], out_dtype=BF16, name="rwkv_a1")
    if v_first is not None:
        tv = _matmul(xv, p["v1"], out_dtype=BF16, name="rwkv_v1")
        tm_v, tn_v = _mm_tiles(M, p["v2"].shape[0], C, None, None)
        tile = pl.BlockSpec((tm_v, tn_v), lambda i, j: (i, j))
        v = _matmul(tv, p["v2"], out_dtype=F32, epilogue=_epi_value_residual,
                    extra=(v, v_first, p["v0"].reshape(1, C).astype(F32)),
                    extra_specs=(tile, tile, pl.BlockSpec((1, tn_v), lambda i, j: (0, j))), name="rwkv_v2")
    tg = _matmul(xg, p["g1"], out_dtype=BF16, epilogue=functools.partial(_epi_plain, act="sigmoid"),
                 name="rwkv_g1")
    g = _matmul(tg, p["g2"], out_dtype=F32, name="rwkv_g2")
    seq = lambda t: t.reshape(B, T, C)
    row = lambda t: t.reshape(1, C).astype(F32)
    low = lambda t: t.reshape(B, T, t.shape[1])
    o_dir = [
        _wkv(seq(r), seq(k), seq(v), low(tw), p["decay2"][d], low(ta), p["a2"][d],
             row(p["w_decay0"][d]), row(p["a0"][d]), row(p["k_k"]), row(p["k_a"]), direction=d)
        for d in range(2)
    ]
    y = _wkv_out(o_dir[0].reshape(M, C), o_dir[1].reshape(M, C), r, k, v, ta, p["a2"], g,
                 p["a0"], p["k_a"], p["r_k"], p["lnx_w"], p["lnx_b"])
    x2 = _matmul_residual(y, p["w_o"], x2, name="rwkv_o")
    return x2, v


def _mla_layer(x, p):
    B, T, C = x.shape
    M = B * T
    x2 = x.reshape(M, C)
    H = p["w_uq"].shape[1] // MLA_QK_PAD
    q_rank = p["q_norm"].shape[0]
    kv_rank = p["kv_norm"].shape[0]
    h = _rmsnorm(x2, p["norm_mix"], out_dtype=BF16)
    lat = _matmul(h, p["w_in"], out_dtype=F32, tn=256, name="mla_in")
    c_q = _rmsnorm(lat, p["q_norm"], out_dtype=BF16, width=q_rank, col_block=0)
    c_kv = _rmsnorm(lat, p["kv_norm"], out_dtype=BF16, width=kv_rank, col_block=q_rank // kv_rank)
    tables = _rope_tables(T, MLA_ROPE_DIM)
    half = MLA_ROPE_DIM // 2
    k_rope = _rope_cols(lat, tables, T, (q_rank + kv_rank) // LANES, half)
    tm = _blk(T, 1024, 8)
    qscale = (MLA_NOPE_DIM + MLA_ROPE_DIM) ** -0.5 * math.log2(math.e)
    q = _matmul(c_q, p["w_uq"], out_dtype=BF16, tm=tm, tn=_blk(H * MLA_QK_PAD, 1024, MLA_QK_PAD),
                epilogue=functools.partial(_epi_rope_second_tile, half=half, qscale=qscale),
                extra=tables, extra_specs=_table_specs(T, _blk(M, tm, 8)), name="mla_uq")
    tm_kv = _blk(M, 1024, 8)
    tn_kv = _blk(H * MLA_V_DIM, 2048, MLA_QK_PAD)
    kv = _matmul(c_kv, p["w_ukv"], out_dtype=BF16, tm=tm_kv, tn=tn_kv,
                 epilogue=functools.partial(_epi_add_second_tile, n_add_blocks=H * MLA_QK_PAD // tn_kv),
                 extra=(k_rope,), extra_specs=(pl.BlockSpec((tm_kv, LANES), lambda i, j: (i, 0)),),
                 name="mla_ukv")
    q3 = q.reshape(B, T, H * MLA_QK_PAD)
    kv3 = kv.reshape(B, T, kv.shape[1])
    o = _flash(q3, kv3, kv3[:, :, H * MLA_QK_PAD:], heads=H, n_maps=1, dqk=MLA_QK_PAD, dv=MLA_V_DIM,
               q_blk0=0, k_blk0=0)
    return _matmul_residual(o.reshape(M, H * MLA_V_DIM), p["w_o"], x2, name="mla_o")


def _diff_layer(x, p, lambda_init):
    B, T, C = x.shape
    M = B * T
    x2 = x.reshape(M, C)
    H = C // (2 * DIFF_HEAD_DIM)
    h = _rmsnorm(x2, p["norm_mix"], out_dtype=BF16)
    tables = _rope_tables(T, DIFF_ROT_DIM)
    tm = _blk(T, 1024, 8)
    tn = _blk(C, 512, LANES)
    qkv = _matmul(h, p["w_qkv"], out_dtype=BF16, tm=tm, tn=tn,
                  epilogue=functools.partial(_epi_rope_tiles, half=DIFF_ROT_DIM // 2,
                                             n_rope_blocks=2 * C // tn, n_q_blocks=C // tn,
                                             qscale=DIFF_HEAD_DIM ** -0.5 * math.log2(math.e)),
                  extra=tables, extra_specs=_table_specs(T, _blk(M, tm, 8)), name="diff_qkv")
    qkv3 = qkv.reshape(B, T, 3 * C)
    o = _flash(qkv3, qkv3, qkv3[:, :, 2 * C:], heads=H, n_maps=2, dqk=DIFF_HEAD_DIM,
               dv=2 * DIFF_HEAD_DIM, q_blk0=0, k_blk0=H,
               diff_params=(p["lambda"].astype(F32), p["subln"].reshape(1, -1).astype(F32)),
               lambda_init=lambda_init)
    return _matmul_residual(o.reshape(M, C), p["w_o"], x2, name="diff_o")


def _trunk(x, layers, final_norm):
    B, T, C = x.shape
    v_first = None
    for i, p in enumerate(layers):
        kind = p["kind"]
        if kind == "rwkv":
            x2, v_layer = _rwkv_layer(x, p, v_first)
            if v_first is None:
                v_first = v_layer
        elif kind == "mla":
            x2 = _mla_layer(x, p)
        else:
            x2 = _diff_layer(x, p, 0.8 - 0.6 * math.exp(-0.3 * i))
        x2 = _ffn(x2, p)
        x = x2.reshape(B, T, C)
    return _rmsnorm(x.reshape(B * T, C), final_norm).reshape(B, T, C)


def _prep_ffn(p):
    return {"norm_ffn": p["norm_ffn"], "w_gu": p["w_gu"].astype(BF16), "w_down": p["w_down"].astype(BF16)}


def _prep_rwkv(p):
    b = lambda w: w.astype(BF16)
    out = dict(kind="rwkv", norm_mix=p["norm_mix"], mix=p["mix"],
               w_r=b(p["w_rkv"][0]), w_k=b(p["w_rkv"][1]), w_v=b(p["w_rkv"][2]),
               w_decay0=p["w_decay0"], a0=p["a0"], k_k=p["k_k"], k_a=p["k_a"], r_k=p["r_k"],
               lnx_w=p["lnx_w"], lnx_b=p["lnx_b"], w_o=b(p["w_o"]))
    for name1, name2, key1, key2 in (("w_decay1", "w_decay2", "decay1", "decay2"), ("a1", "a2", "a1", "a2")):
        w1 = [_pad_to(p[name1][d], 1, LANES) for d in range(2)]
        out[key1] = b(jnp.concatenate(w1, axis=1))
        out[key2] = [b(_pad_to(p[name2][d], 0, LANES)) for d in range(2)]
    if "v0" in p:
        out["v0"] = p["v0"]
        out["v1"] = b(_pad_to(p["v1"], 1, LANES))
        out["v2"] = b(_pad_to(p["v2"], 0, LANES))
    out["g1"] = b(_pad_to(p["g1"], 1, LANES))
    out["g2"] = b(_pad_to(p["g2"], 0, LANES))
    out.update(_prep_ffn(p))
    return out


def _prep_mla(p):
    b = lambda w: w.astype(BF16)
    q_rank = p["q_norm"].shape[0]
    kv_rank = p["kv_norm"].shape[0]
    qk = MLA_NOPE_DIM + MLA_ROPE_DIM
    H = p["w_uq"].shape[1] // qk
    w_uq = _pad_to(p["w_uq"].reshape(q_rank, H, qk), 2, MLA_QK_PAD).reshape(q_rank, H * MLA_QK_PAD)
    w_ukv = p["w_ukv"].reshape(kv_rank, H, MLA_NOPE_DIM + MLA_V_DIM)
    w_k = _pad_to(w_ukv[:, :, :MLA_NOPE_DIM], 2, MLA_QK_PAD).reshape(kv_rank, H * MLA_QK_PAD)
    w_v = w_ukv[:, :, MLA_NOPE_DIM:].reshape(kv_rank, H * MLA_V_DIM)
    out = dict(kind="mla", norm_mix=p["norm_mix"], w_in=b(_pad_to(p["w_in"], 1, 256)),
               q_norm=p["q_norm"], kv_norm=p["kv_norm"], w_uq=b(w_uq),
               w_ukv=b(jnp.concatenate([w_k, w_v], axis=1)), w_o=b(p["w_o"]))
    out.update(_prep_ffn(p))
    return out


def _prep_diff(p):
    out = dict(kind="diff", norm_mix=p["norm_mix"], w_qkv=p["w_qkv"].astype(BF16), subln=p["subln"],
               w_o=p["w_o"].astype(BF16))
    out["lambda"] = p["lambda"]
    out.update(_prep_ffn(p))
    return out


def _forward(x_prompt, x_sample, raw_layers, final_norm):
    prep = {"rwkv": _prep_rwkv, "mla": _prep_mla, "diff": _prep_diff}
    kinds = ("rwkv", "mla", "diff")
    layers = [prep[kinds[i % 3]](p) for i, p in enumerate(raw_layers)]
    return (_trunk(x_prompt, layers, final_norm), _trunk(x_sample, layers, final_norm))


def kernel(x_prompt, x_sample, l0_norm_mix, l0_mix, l0_w_rkv, l0_w_decay0, l0_w_decay1, l0_w_decay2, l0_a0, l0_a1, l0_a2, l0_g1, l0_g2, l0_k_k, l0_k_a, l0_r_k, l0_lnx_w, l0_lnx_b, l0_w_o, l0_norm_ffn, l0_w_gu, l0_w_down, l1_norm_mix, l1_w_in, l1_q_norm, l1_kv_norm, l1_w_uq, l1_w_ukv, l1_w_o, l1_norm_ffn, l1_w_gu, l1_w_down, l2_norm_mix, l2_w_qkv, l2_lambda, l2_subln, l2_w_o, l2_norm_ffn, l2_w_gu, l2_w_down, l3_norm_mix, l3_mix, l3_w_rkv, l3_w_decay0, l3_w_decay1, l3_w_decay2, l3_a0, l3_a1, l3_a2, l3_v0, l3_v1, l3_v2, l3_g1, l3_g2, l3_k_k, l3_k_a, l3_r_k, l3_lnx_w, l3_lnx_b, l3_w_o, l3_norm_ffn, l3_w_gu, l3_w_down, final_norm):
    raw_layers = (
        dict(norm_mix=l0_norm_mix, mix=l0_mix, w_rkv=l0_w_rkv, w_decay0=l0_w_decay0, w_decay1=l0_w_decay1,
             w_decay2=l0_w_decay2, a0=l0_a0, a1=l0_a1, a2=l0_a2, g1=l0_g1, g2=l0_g2, k_k=l0_k_k, k_a=l0_k_a,
             r_k=l0_r_k, lnx_w=l0_lnx_w, lnx_b=l0_lnx_b, w_o=l0_w_o,
             norm_ffn=l0_norm_ffn, w_gu=l0_w_gu, w_down=l0_w_down),
        dict(norm_mix=l1_norm_mix, w_in=l1_w_in, q_norm=l1_q_norm, kv_norm=l1_kv_norm, w_uq=l1_w_uq,
             w_ukv=l1_w_ukv, w_o=l1_w_o, norm_ffn=l1_norm_ffn, w_gu=l1_w_gu, w_down=l1_w_down),
        dict(norm_mix=l2_norm_mix, w_qkv=l2_w_qkv, subln=l2_subln, w_o=l2_w_o,
             norm_ffn=l2_norm_ffn, w_gu=l2_w_gu, w_down=l2_w_down, **{"lambda": l2_lambda}),
        dict(norm_mix=l3_norm_mix, mix=l3_mix, w_rkv=l3_w_rkv, w_decay0=l3_w_decay0, w_decay1=l3_w_decay1,
             w_decay2=l3_w_decay2, a0=l3_a0, a1=l3_a1, a2=l3_a2, v0=l3_v0, v1=l3_v1, v2=l3_v2,
             g1=l3_g1, g2=l3_g2, k_k=l3_k_k, k_a=l3_k_a, r_k=l3_r_k, lnx_w=l3_lnx_w, lnx_b=l3_lnx_b, w_o=l3_w_o,
             norm_ffn=l3_norm_ffn, w_gu=l3_w_gu, w_down=l3_w_down),
    )
    return _forward(x_prompt, x_sample, raw_layers, final_norm)
```

```python
import functools
import math

import jax
import jax.numpy as jnp
from jax import lax
from jax.experimental import pallas as pl
from jax.experimental.pallas import tpu as pltpu

F32 = jnp.float32
BF16 = jnp.bfloat16

LANES = 128
SUBLANES = 8
VMEM_LIMIT = 56 * 1024 * 1024

RMS_EPS = 1e-6
ROPE_THETA = 500000.0
RWKV_HEAD_DIM = 64
RWKV_GN_EPS = 64e-5
WKV_CHUNK = 64
MLA_NOPE_DIM = 128
MLA_ROPE_DIM = 64
MLA_V_DIM = 128
MLA_QK_PAD = 256
MLA_IN_TILE = 256
DIFF_HEAD_DIM = 128
DIFF_ROT_DIM = DIFF_HEAD_DIM // 4
DIFF_EPS = 1e-5

NN = (((1,), (0,)), ((), ()))
NT = (((1,), (1,)), ((), ()))
TN = (((0,), (0,)), ((), ()))


def _blk(n, pref, align):
    if n <= pref:
        return n
    d = (pref // align) * align
    while d >= align:
        if n % d == 0:
            return d
        d -= align
    raise ValueError(f"no block for {n} (pref {pref}, align {align})")


def _params(*sem):
    return pltpu.CompilerParams(dimension_semantics=sem, vmem_limit_bytes=VMEM_LIMIT)


def _pad_to(x, axis, mult):
    n = x.shape[axis]
    pad = (-n) % mult
    if pad == 0:
        return x
    widths = [(0, 0)] * x.ndim
    widths[axis] = (0, pad)
    return jnp.pad(x, widths)


def _rmsnorm_body(x_ref, g_ref, o_ref, *, eps):
    x = x_ref[...].astype(F32)
    y = x * lax.rsqrt(jnp.mean(x * x, axis=-1, keepdims=True) + eps)
    o_ref[...] = (y * g_ref[...]).astype(o_ref.dtype)


def _rmsnorm(x, g, *, eps=RMS_EPS, out_dtype=F32, width=None, col_block=0):
    M = x.shape[0]
    width = x.shape[1] if width is None else width
    tm = _blk(M, 512, SUBLANES)
    return pl.pallas_call(
        functools.partial(_rmsnorm_body, eps=eps),
        out_shape=jax.ShapeDtypeStruct((M, width), out_dtype),
        grid=(M // tm,),
        in_specs=[pl.BlockSpec((tm, width), lambda i: (i, col_block)),
                  pl.BlockSpec((1, width), lambda i: (0, 0))],
        out_specs=pl.BlockSpec((tm, width), lambda i: (i, 0)),
        compiler_params=_params("parallel"),
        name="rmsnorm",
    )(x, g.reshape(1, width).astype(F32))


MIX_R, MIX_W, MIX_K, MIX_V, MIX_A, MIX_G = range(6)


def _premix_body(x_ref, prev_ref, next_ref, g_ref, mix_ref, *refs, eps, has_v1):
    n_w = 4 if has_v1 else 3
    wd1_ref, wa1_ref, wg1_ref = refs[0], refs[1], refs[2]
    wv1_ref = refs[3] if has_v1 else None
    xr_ref, xk_ref, xv_ref, tw_ref, ta_ref, tg_ref = refs[n_w:n_w + 6]
    tv_ref = refs[n_w + 6] if has_v1 else None
    i = pl.program_id(1)
    g = g_ref[...]

    def norm(v):
        return v * lax.rsqrt(jnp.mean(v * v, axis=-1, keepdims=True) + eps) * g

    h = norm(x_ref[0])
    tt = h.shape[0]
    h_before = jnp.where(i == 0, 0.0, norm(prev_ref[0])[SUBLANES - 1:SUBLANES])
    h_after = jnp.where(i == pl.num_programs(1) - 1, 0.0, norm(next_ref[0])[0:1])
    row = lax.broadcasted_iota(jnp.int32, (tt, 1), 0)
    h_dn = jnp.where(row == 0, h_before, pltpu.roll(h, 1, 0))
    h_up = jnp.where(row == tt - 1, h_after, pltpu.roll(h, tt - 1, 0))
    dx = 0.5 * (h_dn + h_up) - h

    def mixed(n):
        return (h + dx * mix_ref[n:n + 1, :]).astype(xr_ref.dtype)

    xr_ref[0] = mixed(MIX_R)
    xk_ref[0] = mixed(MIX_K)
    xv = mixed(MIX_V)
    xv_ref[0] = xv
    tw_ref[0] = jnp.tanh(jnp.dot(mixed(MIX_W), wd1_ref[...], preferred_element_type=F32)).astype(tw_ref.dtype)
    ta_ref[0] = jnp.dot(mixed(MIX_A), wa1_ref[...], preferred_element_type=F32).astype(ta_ref.dtype)
    tg_ref[0] = jax.nn.sigmoid(jnp.dot(mixed(MIX_G), wg1_ref[...], preferred_element_type=F32)).astype(tg_ref.dtype)
    if has_v1:
        tv_ref[0] = jnp.dot(xv, wv1_ref[...], preferred_element_type=F32).astype(tv_ref.dtype)


def _premix(x, g, mix, wd1, wa1, wg1, wv1=None, *, tt=256):
    B, T, C = x.shape
    tt = _blk(T, tt, SUBLANES)
    nb = tt // SUBLANES
    last = T // SUBLANES - 1
    weights = [wd1, wa1, wg1] + ([wv1] if wv1 is not None else [])
    tile = pl.BlockSpec((1, tt, C), lambda b, i: (b, i, 0))
    wide = jax.ShapeDtypeStruct((B, T, C), BF16)
    low_shapes = [jax.ShapeDtypeStruct((B, T, w.shape[1]), BF16) for w in weights]
    low_specs = [pl.BlockSpec((1, tt, w.shape[1]), lambda b, i: (b, i, 0)) for w in weights]
    return pl.pallas_call(
        functools.partial(_premix_body, eps=RMS_EPS, has_v1=wv1 is not None),
        out_shape=[wide] * 3 + low_shapes,
        grid=(B, T // tt),
        in_specs=[tile,
                  pl.BlockSpec((1, SUBLANES, C), lambda b, i: (b, jnp.maximum(i * nb - 1, 0), 0)),
                  pl.BlockSpec((1, SUBLANES, C), lambda b, i: (b, jnp.minimum((i + 1) * nb, last), 0)),
                  pl.BlockSpec((1, C), lambda b, i: (0, 0)),
                  pl.BlockSpec(mix.shape, lambda b, i: (0, 0))]
                 + [pl.BlockSpec(w.shape, lambda b, i: (0, 0)) for w in weights],
        out_specs=[tile] * 3 + low_specs,
        compiler_params=_params("parallel", "parallel"),
        name="rwkv_premix",
    )(x, x, x, g.reshape(1, C).astype(F32), mix.astype(F32), *weights)


def _rope_tables(T, dim):
    half = dim // 2
    inv_freq = 1.0 / (ROPE_THETA ** (jnp.arange(0, dim, 2, dtype=F32) / dim))
    ang = jnp.arange(T, dtype=F32)[:, None] * inv_freq[None, :]
    c, s = jnp.cos(ang), jnp.sin(ang)
    ones = jnp.ones((T, LANES - dim), F32)
    zeros = jnp.zeros((T, LANES - dim), F32)
    zh = jnp.zeros((T, half), F32)
    C = jnp.concatenate([c, c, ones], axis=1)
    S1 = jnp.concatenate([-s, zh, zeros], axis=1)
    S2 = jnp.concatenate([zh, s, zeros], axis=1)
    return C, S1, S2


def _rope_tile(t, c, s1, s2, half):
    return t * c + pltpu.roll(t, LANES - half, 1) * s1 + pltpu.roll(t, half, 1) * s2


def _rope_cols_body(x_ref, c_ref, s1_ref, s2_ref, o_ref, *, half):
    o_ref[...] = _rope_tile(x_ref[...].astype(F32), c_ref[...], s1_ref[...], s2_ref[...], half)


def _rope_cols(x, tables, T, col_block, half):
    M = x.shape[0]
    tm = _blk(T, 512, SUBLANES)
    nt = T // tm
    tab = pl.BlockSpec((tm, LANES), lambda i: (i % nt, 0))
    return pl.pallas_call(
        functools.partial(_rope_cols_body, half=half),
        out_shape=jax.ShapeDtypeStruct((M, LANES), F32),
        grid=(M // tm,),
        in_specs=[pl.BlockSpec((tm, LANES), lambda i: (i, col_block)), tab, tab, tab],
        out_specs=pl.BlockSpec((tm, LANES), lambda i: (i, 0)),
        compiler_params=_params("parallel"),
        name="rope_cols",
    )(x, *tables)


def _mm_body(*refs, n_extra, epilogue):
    x_ref, w_ref = refs[0], refs[1]
    extra = refs[2:2 + n_extra]
    o_ref = refs[2 + n_extra]
    acc = jnp.dot(x_ref[...], w_ref[...], preferred_element_type=F32)
    epilogue(acc, extra, o_ref)


def _epi_plain(acc, extra, o_ref):
    o_ref[...] = acc.astype(o_ref.dtype)


def _epi_residual(acc, extra, o_ref):
    o_ref[...] = (extra[0][...] + acc).astype(o_ref.dtype)


def _epi_value_residual(acc, extra, o_ref):
    v = extra[0][...]
    o_ref[...] = (v + (extra[1][...] - v) * jax.nn.sigmoid(extra[2][...] + acc)).astype(o_ref.dtype)


def _epi_rope_second_tile(acc, extra, o_ref, *, half, qscale):
    c, s1, s2 = (e[...] for e in extra)
    acc = acc * qscale
    for t in range(acc.shape[1] // LANES):
        sl = slice(t * LANES, (t + 1) * LANES)
        tile = _rope_tile(acc[:, sl], c, s1, s2, half) if t % 2 else acc[:, sl]
        o_ref[:, sl] = tile.astype(o_ref.dtype)


def _epi_rope_tiles(acc, extra, o_ref, *, half, n_rope_blocks, n_q_blocks, qscale):
    j = pl.program_id(1)

    @pl.when(j < n_rope_blocks)
    def _():
        c, s1, s2 = (e[...] for e in extra)
        scaled = acc * jnp.where(j < n_q_blocks, qscale, 1.0).astype(F32)
        for t in range(acc.shape[1] // LANES):
            sl = slice(t * LANES, (t + 1) * LANES)
            o_ref[:, sl] = _rope_tile(scaled[:, sl], c, s1, s2, half).astype(o_ref.dtype)

    @pl.when(j >= n_rope_blocks)
    def _():
        o_ref[...] = acc.astype(o_ref.dtype)


def _epi_add_second_tile(acc, extra, o_ref, *, n_add_blocks):
    j = pl.program_id(1)

    @pl.when(j < n_add_blocks)
    def _():
        shared = extra[0][...]
        for t in range(acc.shape[1] // LANES):
            sl = slice(t * LANES, (t + 1) * LANES)
            tile = acc[:, sl] + shared if t % 2 else acc[:, sl]
            o_ref[:, sl] = tile.astype(o_ref.dtype)

    @pl.when(j >= n_add_blocks)
    def _():
        o_ref[...] = acc.astype(o_ref.dtype)


X_TILE_BYTES = 8 * 1024 * 1024


def _mm_tiles(M, K, N, tm, tn):
    if tm is None:
        tm = 1024 if 1024 * K * 2 <= X_TILE_BYTES else 512
    if tn is None:
        tn = 512
    return _blk(M, tm, SUBLANES), _blk(N, tn, LANES)


def _matmul(x, w, *, out_dtype, epilogue=None, extra=(), extra_specs=(), tm=None, tn=None,
            name="matmul"):
    M = x.shape[0]
    K, N = w.shape
    tm, tn = _mm_tiles(M, K, N, tm, tn)
    if epilogue is None:
        epilogue = _epi_plain
    return pl.pallas_call(
        functools.partial(_mm_body, n_extra=len(extra), epilogue=epilogue),
        out_shape=jax.ShapeDtypeStruct((M, N), out_dtype),
        grid=(M // tm, N // tn),
        in_specs=[pl.BlockSpec((tm, K), lambda i, j: (i, 0)),
                  pl.BlockSpec((K, tn), lambda i, j: (0, j))] + list(extra_specs),
        out_specs=pl.BlockSpec((tm, tn), lambda i, j: (i, j)),
        compiler_params=_params("parallel", "arbitrary"),
        name=name,
    )(x, w, *extra)


def _matmul_residual(x, w, res, *, tm=None, tn=None, name="matmul_residual"):
    M = x.shape[0]
    K, N = w.shape
    tm_, tn_ = _mm_tiles(M, K, N, tm, tn)
    return _matmul(x, w, out_dtype=F32, epilogue=_epi_residual, extra=(res,),
                   extra_specs=(pl.BlockSpec((tm_, tn_), lambda i, j: (i, j)),), tm=tm, tn=tn, name=name)


def _table_specs(T, tm):
    nt = T // tm
    return tuple(pl.BlockSpec((tm, LANES), lambda i, j: (i % nt, 0)) for _ in range(3))


def _swiglu_body(x_ref, wg_ref, wu_ref, o_ref):
    x = x_ref[...]
    g = jnp.dot(x, wg_ref[...], preferred_element_type=F32)
    u = jnp.dot(x, wu_ref[...], preferred_element_type=F32)
    o_ref[...] = (g * jax.nn.sigmoid(g) * u).astype(o_ref.dtype)


def _swiglu_up(x, w_gu, hidden, *, tm=2048, tn=256):
    M, K = x.shape
    tm = _blk(M, tm, SUBLANES)
    tn = _blk(hidden, tn, LANES)
    nb = hidden // tn
    return pl.pallas_call(
        _swiglu_body,
        out_shape=jax.ShapeDtypeStruct((M, hidden), BF16),
        grid=(M // tm, nb),
        in_specs=[pl.BlockSpec((tm, K), lambda i, j: (i, 0), pipeline_mode=pl.Buffered(1)),
                  pl.BlockSpec((K, tn), lambda i, j: (0, j)),
                  pl.BlockSpec((K, tn), lambda i, j: (0, j + nb))],
        out_specs=pl.BlockSpec((tm, tn), lambda i, j: (i, j)),
        compiler_params=_params("parallel", "arbitrary"),
        name="swiglu_up",
    )(x, w_gu, w_gu)


def _flash_body(*refs, n_maps, dqk, dv, diff, lambda_init, bks):
    if diff:
        q_ref, k_ref, vt_ref, lam_ref, g_ref, o_ref, m_scr, acc_scr = refs
    else:
        q_ref, k_ref, vt_ref, o_ref, m_scr, acc_scr = refs
    j = pl.program_id(3)

    @pl.when(j == 0)
    def _():
        m_scr[...] = jnp.full(m_scr.shape, -jnp.inf, F32)
        acc_scr[...] = jnp.zeros(acc_scr.shape, F32)

    q = q_ref[0]
    maps = [slice(mp * dqk, (mp + 1) * dqk) for mp in range(n_maps)]
    m = [m_scr[mp] for mp in range(n_maps)]
    acc = [acc_scr[mp] for mp in range(n_maps)]
    n_sub = k_ref.shape[1] // bks
    st = []
    for sub in range(n_sub):
        k = k_ref[0, sub * bks:(sub + 1) * bks, :]
        st.append([lax.dot_general(k[:, sl], q[:, sl], NT, preferred_element_type=F32) for sl in maps])
    for sub in range(n_sub):
        vt = vt_ref[0, :, sub * bks:(sub + 1) * bks]
        for mp in range(n_maps):
            s_t = st[sub][mp]
            m_new = jnp.maximum(m[mp], jnp.max(s_t, axis=0, keepdims=True))
            alpha = jnp.exp2(m[mp] - m_new)
            p = jnp.exp2(s_t - m_new).astype(vt.dtype)
            acc[mp] = alpha * acc[mp] + jnp.dot(vt, p, preferred_element_type=F32)
            m[mp] = m_new
    for mp in range(n_maps):
        m_scr[mp] = m[mp]
        acc_scr[mp] = acc[mp]

    @pl.when(j == pl.num_programs(3) - 1)
    def _():
        o = [(a[:dv] / a[dv:dv + 1]).T for a in acc]
        if diff:
            lp = lam_ref[...]
            lam = (jnp.exp(jnp.sum(lp[0:1] * lp[1:2], axis=-1, keepdims=True))
                   - jnp.exp(jnp.sum(lp[2:3] * lp[3:4], axis=-1, keepdims=True)) + lambda_init)
            od = o[0] - lam * o[1]
            y = od * lax.rsqrt(jnp.mean(od * od, axis=-1, keepdims=True) + DIFF_EPS) * g_ref[...]
            o_ref[0] = (y * (1.0 - lambda_init)).astype(o_ref.dtype)
        else:
            o_ref[0] = o[0].astype(o_ref.dtype)


ONES_ROWS = 16


def _v_transposed(v, heads, dv):
    B, T = v.shape[0], v.shape[1]
    vt = jnp.swapaxes(v, 1, 2).reshape(B, heads, dv, T)
    ones = jnp.ones((B, heads, ONES_ROWS, T), v.dtype)
    return jnp.concatenate([vt, ones], axis=2).reshape(B, heads * (dv + ONES_ROWS), T)


def _flash(q_arr, k_arr, v_arr, *, heads, n_maps, dqk, dv, q_blk0, k_blk0,
           diff_params=None, lambda_init=0.0, bq=512, bk=8192, bks=256):
    B, T = q_arr.shape[0], q_arr.shape[1]
    bq = _blk(T, bq, LANES)
    bk = _blk(T, bk, LANES)
    bks = _blk(bk, bks, LANES)
    wqk = n_maps * dqk
    dve = dv + ONES_ROWS
    diff = diff_params is not None
    in_specs = [pl.BlockSpec((1, bq, wqk), lambda b, h, i, j: (b, i, q_blk0 + h)),
                pl.BlockSpec((1, bk, wqk), lambda b, h, i, j: (b, j, k_blk0 + h)),
                pl.BlockSpec((1, dve, bk), lambda b, h, i, j: (b, h, j))]
    args = [q_arr, k_arr, _v_transposed(v_arr, heads, dv)]
    if diff:
        lam, subln = diff_params
        in_specs += [pl.BlockSpec(lam.shape, lambda b, h, i, j: (0, 0)),
                     pl.BlockSpec(subln.shape, lambda b, h, i, j: (0, 0))]
        args += [lam, subln]
    return pl.pallas_call(
        functools.partial(_flash_body, n_maps=n_maps, dqk=dqk, dv=dv, diff=diff,
                          lambda_init=lambda_init, bks=bks),
        out_shape=jax.ShapeDtypeStruct((B, T, heads * dv), BF16),
        grid=(B, heads, T // bq, T // bk),
        in_specs=in_specs,
        out_specs=pl.BlockSpec((1, bq, dv), lambda b, h, i, j: (b, i, h)),
        scratch_shapes=[pltpu.VMEM((n_maps, 1, bq), F32),
                        pltpu.VMEM((n_maps, dve, bq), F32)],
        compiler_params=_params("parallel", "parallel", "parallel", "arbitrary"),
        name="flash_diff" if diff else "flash_mla",
    )(*args)


def _split_dot(m_exact, x, n_parts):
    acc = None
    rem = x
    for _ in range(n_parts):
        part = rem.astype(BF16)
        rem = rem - part.astype(F32)
        d = lax.dot_general(m_exact, part, NN, preferred_element_type=F32)
        acc = d if acc is None else acc + d
    return acc


def _wkv_body(r_ref, k_ref, v_ref, tw_ref, wd2_ref, ta_ref, wa2_ref, w0_ref, a0_ref, kk_ref, ka_ref,
              o_ref, s_scr,
              *, reverse, L, pairs):
    c = pl.program_id(2)

    @pl.when(c == 0)
    def _():
        s_scr[...] = jnp.zeros(s_scr.shape, F32)

    L2 = 2 * L
    ri = lax.broadcasted_iota(jnp.int32, (L, L), 0)
    ci = lax.broadcasted_iota(jnp.int32, (L, L), 1)
    tri = ((ri <= ci) if reverse else (ri >= ci)).astype(BF16)
    r2 = lax.broadcasted_iota(jnp.int32, (L2, L2), 0)
    c2 = lax.broadcasted_iota(jnp.int32, (L2, L2), 1)
    strict = (r2 < c2) if reverse else (r2 > c2)
    incl = (r2 <= c2) if reverse else (r2 >= c2)
    eye = jnp.where(r2 == c2, 1.0, 0.0).astype(F32)
    lane = lax.broadcasted_iota(jnp.int32, (L, LANES), 1)
    head0 = lane < RWKV_HEAD_DIM
    hr = lax.broadcasted_iota(jnp.int32, (LANES, LANES), 0) // RWKV_HEAD_DIM
    hc = lax.broadcasted_iota(jnp.int32, (LANES, LANES), 1) // RWKV_HEAD_DIM
    head_ones = jnp.where(hr == hc, 1.0, 0.0).astype(BF16)

    def stack(q):
        return jnp.concatenate([jnp.where(head0, q, 0.0), jnp.where(head0, 0.0, q)], axis=0).astype(BF16)

    dotf = lambda a, b, dims: lax.dot_general(a, b, dims, preferred_element_type=F32)
    P = range(pairs)
    lanes = [slice(p * LANES, (p + 1) * LANES) for p in P]

    zw = jnp.dot(tw_ref[0], wd2_ref[...], preferred_element_type=F32)
    za = jnp.dot(ta_ref[0], wa2_ref[...], preferred_element_type=F32)
    lw = -math.exp(-0.5) * jax.nn.sigmoid(w0_ref[...] + zw)
    asig = jax.nn.sigmoid(a0_ref[...] + za)
    k_all = k_ref[0]
    kkr = k_all * kk_ref[...]
    kkr2 = kkr * kkr
    n2 = jnp.concatenate([_head_sum(kkr2[:, sl], head_ones) for sl in lanes], axis=1)
    kk = kkr / jnp.maximum(jnp.sqrt(n2), 1e-12)
    kd = k_all * (1.0 + (asig - 1.0) * ka_ref[...])
    b_all = kk * asig
    cum = _split_dot(tri, lw, 3)
    tot = jnp.sum(lw, axis=0, keepdims=True)
    a_t = -kk * jnp.exp(cum - lw)
    r_t = r_ref[0] * jnp.exp(cum)
    pinv = jnp.exp(-cum)
    pend = jnp.exp(tot - cum)
    decay = jnp.exp(tot)
    b_in, k_in = b_all * pinv, kd * pinv
    b_end, k_end = b_all * pend, kd * pend

    As = [stack(a_t[:, sl]) for sl in lanes]
    Bs = [stack(b_in[:, sl]) for sl in lanes]
    m_ab = [jnp.where(strict, dotf(As[p], Bs[p], NT), 0.0) for p in P]

    inv = None
    s = 1
    while s < L:
        sh = s.bit_length() - 1
        off_diag = ((r2 >> (sh + 1)) == (c2 >> (sh + 1))) & ((r2 >> sh) != (c2 >> sh))
        e = [jnp.where(off_diag, m_ab[p], 0.0) for p in P]
        if s == 1:
            inv = [eye + e[p] for p in P]
        else:
            invb = [inv[p].astype(BF16) for p in P]
            t = [dotf(e[p].astype(BF16), invb[p], NN).astype(BF16) for p in P]
            inv = [inv[p] + dotf(invb[p], t[p], NN) for p in P]
        s *= 2

    Rs = [stack(r_t[:, sl]) for sl in lanes]
    Ks = [stack(k_in[:, sl]) for sl in lanes]
    Vs = [stack(v_ref[0, :, sl]) for sl in lanes]
    m_ak = [jnp.where(strict, dotf(As[p], Ks[p], NT), 0.0).astype(BF16) for p in P]
    m_r = [jnp.concatenate([jnp.where(incl, dotf(Rs[p], Bs[p], NT), 0.0),
                            jnp.where(incl, dotf(Rs[p], Ks[p], NT), 0.0)], axis=1).astype(BF16) for p in P]
    state = [s_scr[p] for p in P]
    sb = [state[p].astype(BF16) for p in P]
    w_s = [dotf(As[p], sb[p], NT) + dotf(m_ak[p], Vs[p], NN) for p in P]
    u_s = [dotf(inv[p].astype(BF16), w_s[p].astype(BF16), NN).astype(BF16) for p in P]
    uv = [jnp.concatenate([u_s[p], Vs[p]], axis=0) for p in P]
    o_s = [dotf(Rs[p], sb[p], NT) + dotf(m_r[p], uv[p], NN) for p in P]
    bk_end = [jnp.concatenate([stack(b_end[:, sl]), stack(k_end[:, sl])], axis=0) for sl in lanes]
    new_state = [state[p] * decay[:, lanes[p]] + dotf(uv[p], bk_end[p], TN) for p in P]
    o_ref[0] = jnp.concatenate([o[:L] + o[L:] for o in o_s], axis=1)
    s_scr[...] = jnp.stack(new_state)


def _head_sum(x, head_ones):
    hi = x.astype(BF16)
    lo = (x - hi.astype(F32)).astype(BF16)
    return (lax.dot_general(hi, head_ones, NN, preferred_element_type=F32)
            + lax.dot_general(lo, head_ones, NN, preferred_element_type=F32))


def _wkv(r, k, v, tw, wd2, ta, wa2, w0, a0, k_k, k_a, *, direction, pairs=32):
    B, T, C = r.shape
    L = WKV_CHUNK
    pairs = min(pairs, C // LANES)
    wl = pairs * LANES
    nc = T // L
    R = wd2.shape[0]
    reverse = direction == 1

    def chunk(c):
        return (nc - 1 - c) if reverse else c

    seq = pl.BlockSpec((1, L, wl), lambda b, g, c: (b, chunk(c), g))
    low = pl.BlockSpec((1, L, R), lambda b, g, c: (b, chunk(c), direction))
    fac = pl.BlockSpec((R, wl), lambda b, g, c: (0, g))
    par = pl.BlockSpec((1, wl), lambda b, g, c: (0, g))
    return pl.pallas_call(
        functools.partial(_wkv_body, reverse=reverse, L=L, pairs=pairs),
        out_shape=jax.ShapeDtypeStruct((B, T, C), F32),
        grid=(B, C // wl, nc),
        in_specs=[seq, seq, seq, low, fac, low, fac, par, par, par, par],
        out_specs=seq,
        scratch_shapes=[pltpu.VMEM((pairs, LANES, LANES), F32)],
        compiler_params=_params("parallel", "parallel", "arbitrary"),
        name="wkv7_rev" if reverse else "wkv7_fwd",
    )(r, k, v, tw, wd2, ta, wa2, w0, a0, k_k, k_a)


def _wkv_out_body(of_ref, ob_ref, r_ref, k_ref, v_ref, ta_ref, wa20_ref, wa21_ref, g_ref,
                  a00_ref, a01_ref, ka_ref, rk_ref, lw_ref, lb_ref, o_ref):
    hr = lax.broadcasted_iota(jnp.int32, (LANES, LANES), 0) // RWKV_HEAD_DIM
    hc = lax.broadcasted_iota(jnp.int32, (LANES, LANES), 1) // RWKV_HEAD_DIM
    head_ones = jnp.where(hr == hc, 1.0, 0.0).astype(BF16)
    inv_n = 1.0 / RWKV_HEAD_DIM
    R = wa20_ref.shape[0]
    za0 = jnp.dot(ta_ref[:, :R], wa20_ref[...], preferred_element_type=F32)
    za1 = jnp.dot(ta_ref[:, R:], wa21_ref[...], preferred_element_type=F32)
    asum_all = jax.nn.sigmoid(a00_ref[...] + za0) + jax.nn.sigmoid(a01_ref[...] + za1)
    for t in range(o_ref.shape[1] // LANES):
        sl = slice(t * LANES, (t + 1) * LANES)
        o = of_ref[:, sl] + ob_ref[:, sl]
        mu = _head_sum(o, head_ones) * inv_n
        d = o - mu
        var = _head_sum(d * d, head_ones) * inv_n
        on = d * lax.rsqrt(var + RWKV_GN_EPS) * lw_ref[:, sl] + lb_ref[:, sl]
        asum = asum_all[:, sl]
        ksum = k_ref[:, sl] * (2.0 + (asum - 2.0) * ka_ref[:, sl])
        bonus = _head_sum(r_ref[:, sl] * ksum * rk_ref[:, sl], head_ones) * v_ref[:, sl]
        o_ref[:, sl] = ((on + bonus) * g_ref[:, sl]).astype(o_ref.dtype)


def _wkv_out(o_f, o_b, r, k, v, ta, wa2, g, a0, k_a, r_k, lnx_w, lnx_b):
    M, C = r.shape
    tm = _blk(M, 512, SUBLANES)
    tc = _blk(C, 512, LANES)
    R = wa2[0].shape[0]
    seq = pl.BlockSpec((tm, tc), lambda i, j: (i, j))
    low = pl.BlockSpec((tm, 2 * R), lambda i, j: (i, 0))
    fac = pl.BlockSpec((R, tc), lambda i, j: (0, j))
    par = pl.BlockSpec((1, tc), lambda i, j: (0, j))
    row = lambda x: x.reshape(1, C).astype(F32)
    return pl.pallas_call(
        _wkv_out_body,
        out_shape=jax.ShapeDtypeStruct((M, C), BF16),
        grid=(M // tm, C // tc),
        in_specs=[seq] * 5 + [low, fac, fac, seq] + [par] * 6,
        out_specs=seq,
        compiler_params=_params("parallel", "parallel"),
        name="wkv_out",
    )(o_f, o_b, r, k, v, ta, wa2[0], wa2[1], g,
      row(a0[0]), row(a0[1]), row(k_a), row(r_k), row(lnx_w), row(lnx_b))


def _ffn(x2, p):
    h = _rmsnorm(x2, p["norm_ffn"], out_dtype=BF16)
    hidden = p["w_down"].shape[0]
    mid = _swiglu_up(h, p["w_gu"], hidden)
    return _matmul_residual(mid, p["w_down"], x2, tm=512, tn=256, name="ffn_down")


def _rwkv_layer(x, p, v_first):
    B, T, C = x.shape
    M = B * T
    x2 = x.reshape(M, C)
    flat = lambda t: t.reshape(M, t.shape[2])
    xr, xk, xv, tw, ta, tg, *tv = map(flat, _premix(x, p["norm_mix"], p["mix"], p["decay1"], p["a1"], p["g1"],
                                                      p.get("v1")))
    r = _matmul(xr, p["w_r"], out_dtype=F32, name="rwkv_r")
    k = _matmul(xk, p["w_k"], out_dtype=F32, name="rwkv_k")
    v = _matmul(xv, p["w_v"], out_dtype=F32, name="rwkv_v")
    if v_first is not None:
        tv = tv[0]
        tm_v, tn_v = _mm_tiles(M, p["v2"].shape[0], C, None, None)
        tile = pl.BlockSpec((tm_v, tn_v), lambda i, j: (i, j))
        v = _matmul(tv, p["v2"], out_dtype=F32, epilogue=_epi_value_residual,
                    extra=(v, v_first, p["v0"].reshape(1, C).astype(F32)),
                    extra_specs=(tile, tile, pl.BlockSpec((1, tn_v), lambda i, j: (0, j))), name="rwkv_v2")
    g = _matmul(tg, p["g2"], out_dtype=F32, name="rwkv_g2")
    seq = lambda t: t.reshape(B, T, C)
    row = lambda t: t.reshape(1, C).astype(F32)
    low = lambda t: t.reshape(B, T, t.shape[1])
    o_dir = [
        _wkv(seq(r), seq(k), seq(v), low(tw), p["decay2"][d], low(ta), p["a2"][d],
             row(p["w_decay0"][d]), row(p["a0"][d]), row(p["k_k"]), row(p["k_a"]), direction=d)
        for d in range(2)
    ]
    y = _wkv_out(o_dir[0].reshape(M, C), o_dir[1].reshape(M, C), r, k, v, ta, p["a2"], g,
                 p["a0"], p["k_a"], p["r_k"], p["lnx_w"], p["lnx_b"])
    x2 = _matmul_residual(y, p["w_o"], x2, name="rwkv_o")
    return x2, v


def _mla_layer(x, p):
    B, T, C = x.shape
    M = B * T
    x2 = x.reshape(M, C)
    H = p["w_uq"].shape[1] // MLA_QK_PAD
    q_rank = p["q_norm"].shape[0]
    kv_rank = p["kv_norm"].shape[0]
    h = _rmsnorm(x2, p["norm_mix"], out_dtype=BF16)
    lat = _matmul(h, p["w_in"], out_dtype=F32, tn=MLA_IN_TILE, name="mla_in")
    c_q = _rmsnorm(lat, p["q_norm"], out_dtype=BF16, width=q_rank, col_block=0)
    c_kv = _rmsnorm(lat, p["kv_norm"], out_dtype=BF16, width=kv_rank, col_block=q_rank // kv_rank)
    tables = _rope_tables(T, MLA_ROPE_DIM)
    half = MLA_ROPE_DIM // 2
    k_rope = _rope_cols(lat, tables, T, (q_rank + kv_rank) // LANES, half)
    tm = _blk(T, 1024, SUBLANES)
    qscale = (MLA_NOPE_DIM + MLA_ROPE_DIM) ** -0.5 * math.log2(math.e)
    q = _matmul(c_q, p["w_uq"], out_dtype=BF16, tm=tm, tn=_blk(H * MLA_QK_PAD, 1024, MLA_QK_PAD),
                epilogue=functools.partial(_epi_rope_second_tile, half=half, qscale=qscale),
                extra=tables, extra_specs=_table_specs(T, _blk(M, tm, SUBLANES)), name="mla_uq")
    tm_kv = _blk(M, 1024, SUBLANES)
    tn_kv = _blk(H * MLA_V_DIM, 2048, MLA_QK_PAD)
    kv = _matmul(c_kv, p["w_ukv"], out_dtype=BF16, tm=tm_kv, tn=tn_kv,
                 epilogue=functools.partial(_epi_add_second_tile, n_add_blocks=H * MLA_QK_PAD // tn_kv),
                 extra=(k_rope,), extra_specs=(pl.BlockSpec((tm_kv, LANES), lambda i, j: (i, 0)),),
                 name="mla_ukv")
    q3 = q.reshape(B, T, H * MLA_QK_PAD)
    kv3 = kv.reshape(B, T, kv.shape[1])
    o = _flash(q3, kv3, kv3[:, :, H * MLA_QK_PAD:], heads=H, n_maps=1, dqk=MLA_QK_PAD, dv=MLA_V_DIM,
               q_blk0=0, k_blk0=0)
    return _matmul_residual(o.reshape(M, H * MLA_V_DIM), p["w_o"], x2, name="mla_o")


def _diff_layer(x, p, lambda_init):
    B, T, C = x.shape
    M = B * T
    x2 = x.reshape(M, C)
    H = C // (2 * DIFF_HEAD_DIM)
    h = _rmsnorm(x2, p["norm_mix"], out_dtype=BF16)
    tables = _rope_tables(T, DIFF_ROT_DIM)
    tm = _blk(T, 1024, SUBLANES)
    tn = _blk(C, 512, LANES)
    qkv = _matmul(h, p["w_qkv"], out_dtype=BF16, tm=tm, tn=tn,
                  epilogue=functools.partial(_epi_rope_tiles, half=DIFF_ROT_DIM // 2,
                                             n_rope_blocks=2 * C // tn, n_q_blocks=C // tn,
                                             qscale=DIFF_HEAD_DIM ** -0.5 * math.log2(math.e)),
                  extra=tables, extra_specs=_table_specs(T, _blk(M, tm, SUBLANES)), name="diff_qkv")
    qkv3 = qkv.reshape(B, T, 3 * C)
    o = _flash(qkv3, qkv3, qkv3[:, :, 2 * C:], heads=H, n_maps=2, dqk=DIFF_HEAD_DIM,
               dv=2 * DIFF_HEAD_DIM, q_blk0=0, k_blk0=H,
               diff_params=(p["lambda"].astype(F32), p["subln"].reshape(1, -1).astype(F32)),
               lambda_init=lambda_init)
    return _matmul_residual(o.reshape(M, C), p["w_o"], x2, name="diff_o")


def _trunk(x, layers, final_norm):
    B, T, C = x.shape
    v_first = None
    for i, p in enumerate(layers):
        kind = p["kind"]
        if kind == "rwkv":
            x2, v_layer = _rwkv_layer(x, p, v_first)
            if v_first is None:
                v_first = v_layer
        elif kind == "mla":
            x2 = _mla_layer(x, p)
        else:
            x2 = _diff_layer(x, p, 0.8 - 0.6 * math.exp(-0.3 * i))
        x2 = _ffn(x2, p)
        x = x2.reshape(B, T, C)
    return _rmsnorm(x.reshape(B * T, C), final_norm).reshape(B, T, C)


def _prep_ffn(p):
    return {"norm_ffn": p["norm_ffn"], "w_gu": p["w_gu"].astype(BF16), "w_down": p["w_down"].astype(BF16)}


def _prep_rwkv(p):
    b = lambda w: w.astype(BF16)
    out = dict(kind="rwkv", norm_mix=p["norm_mix"], mix=p["mix"],
               w_r=b(p["w_rkv"][0]), w_k=b(p["w_rkv"][1]), w_v=b(p["w_rkv"][2]),
               w_decay0=p["w_decay0"], a0=p["a0"], k_k=p["k_k"], k_a=p["k_a"], r_k=p["r_k"],
               lnx_w=p["lnx_w"], lnx_b=p["lnx_b"], w_o=b(p["w_o"]))
    for name1, name2, key1, key2 in (("w_decay1", "w_decay2", "decay1", "decay2"), ("a1", "a2", "a1", "a2")):
        w1 = [_pad_to(p[name1][d], 1, LANES) for d in range(2)]
        out[key1] = b(jnp.concatenate(w1, axis=1))
        out[key2] = [b(_pad_to(p[name2][d], 0, LANES)) for d in range(2)]
    if "v0" in p:
        out["v0"] = p["v0"]
        out["v1"] = b(_pad_to(p["v1"], 1, LANES))
        out["v2"] = b(_pad_to(p["v2"], 0, LANES))
    out["g1"] = b(_pad_to(p["g1"], 1, LANES))
    out["g2"] = b(_pad_to(p["g2"], 0, LANES))
    out.update(_prep_ffn(p))
    return out


def _prep_mla(p):
    b = lambda w: w.astype(BF16)
    q_rank = p["q_norm"].shape[0]
    kv_rank = p["kv_norm"].shape[0]
    qk = MLA_NOPE_DIM + MLA_ROPE_DIM
    H = p["w_uq"].shape[1] // qk
    w_uq = _pad_to(p["w_uq"].reshape(q_rank, H, qk), 2, MLA_QK_PAD).reshape(q_rank, H * MLA_QK_PAD)
    w_ukv = p["w_ukv"].reshape(kv_rank, H, MLA_NOPE_DIM + MLA_V_DIM)
    w_k = _pad_to(w_ukv[:, :, :MLA_NOPE_DIM], 2, MLA_QK_PAD).reshape(kv_rank, H * MLA_QK_PAD)
    w_v = w_ukv[:, :, MLA_NOPE_DIM:].reshape(kv_rank, H * MLA_V_DIM)
    out = dict(kind="mla", norm_mix=p["norm_mix"], w_in=b(_pad_to(p["w_in"], 1, MLA_IN_TILE)),
               q_norm=p["q_norm"], kv_norm=p["kv_norm"], w_uq=b(w_uq),
               w_ukv=b(jnp.concatenate([w_k, w_v], axis=1)), w_o=b(p["w_o"]))
    out.update(_prep_ffn(p))
    return out


def _prep_diff(p):
    out = dict(kind="diff", norm_mix=p["norm_mix"], w_qkv=p["w_qkv"].astype(BF16), subln=p["subln"],
               w_o=p["w_o"].astype(BF16))
    out["lambda"] = p["lambda"]
    out.update(_prep_ffn(p))
    return out


def _forward(x_prompt, x_sample, raw_layers, final_norm):
    prep = {"rwkv": _prep_rwkv, "mla": _prep_mla, "diff": _prep_diff}
    kinds = ("rwkv", "mla", "diff")
    layers = [prep[kinds[i % 3]](p) for i, p in enumerate(raw_layers)]
    return (_trunk(x_prompt, layers, final_norm), _trunk(x_sample, layers, final_norm))


def kernel(x_prompt, x_sample, l0_norm_mix, l0_mix, l0_w_rkv, l0_w_decay0, l0_w_decay1, l0_w_decay2, l0_a0, l0_a1, l0_a2, l0_g1, l0_g2, l0_k_k, l0_k_a, l0_r_k, l0_lnx_w, l0_lnx_b, l0_w_o, l0_norm_ffn, l0_w_gu, l0_w_down, l1_norm_mix, l1_w_in, l1_q_norm, l1_kv_norm, l1_w_uq, l1_w_ukv, l1_w_o, l1_norm_ffn, l1_w_gu, l1_w_down, l2_norm_mix, l2_w_qkv, l2_lambda, l2_subln, l2_w_o, l2_norm_ffn, l2_w_gu, l2_w_down, l3_norm_mix, l3_mix, l3_w_rkv, l3_w_decay0, l3_w_decay1, l3_w_decay2, l3_a0, l3_a1, l3_a2, l3_v0, l3_v1, l3_v2, l3_g1, l3_g2, l3_k_k, l3_k_a, l3_r_k, l3_lnx_w, l3_lnx_b, l3_w_o, l3_norm_ffn, l3_w_gu, l3_w_down, final_norm):
    raw_layers = (
        dict(norm_mix=l0_norm_mix, mix=l0_mix, w_rkv=l0_w_rkv, w_decay0=l0_w_decay0, w_decay1=l0_w_decay1,
             w_decay2=l0_w_decay2, a0=l0_a0, a1=l0_a1, a2=l0_a2, g1=l0_g1, g2=l0_g2, k_k=l0_k_k, k_a=l0_k_a,
             r_k=l0_r_k, lnx_w=l0_lnx_w, lnx_b=l0_lnx_b, w_o=l0_w_o,
             norm_ffn=l0_norm_ffn, w_gu=l0_w_gu, w_down=l0_w_down),
        dict(norm_mix=l1_norm_mix, w_in=l1_w_in, q_norm=l1_q_norm, kv_norm=l1_kv_norm, w_uq=l1_w_uq,
             w_ukv=l1_w_ukv, w_o=l1_w_o, norm_ffn=l1_norm_ffn, w_gu=l1_w_gu, w_down=l1_w_down),
        dict(norm_mix=l2_norm_mix, w_qkv=l2_w_qkv, subln=l2_subln, w_o=l2_w_o,
             norm_ffn=l2_norm_ffn, w_gu=l2_w_gu, w_down=l2_w_down, **{"lambda": l2_lambda}),
        dict(norm_mix=l3_norm_mix, mix=l3_mix, w_rkv=l3_w_rkv, w_decay0=l3_w_decay0, w_decay1=l3_w_decay1,
             w_decay2=l3_w_decay2, a0=l3_a0, a1=l3_a1, a2=l3_a2, v0=l3_v0, v1=l3_v1, v2=l3_v2,
             g1=l3_g1, g2=l3_g2, k_k=l3_k_k, k_a=l3_k_a, r_k=l3_r_k, lnx_w=l3_lnx_w, lnx_b=l3_lnx_b, w_o=l3_w_o,
             norm_ffn=l3_norm_ffn, w_gu=l3_w_gu, w_down=l3_w_down),
    )
    return _forward(x_prompt, x_sample, raw_layers, final_norm)
```

```python
import functools
import math

import jax
import jax.numpy as jnp
from jax import lax
from jax.experimental import pallas as pl
from jax.experimental.pallas import tpu as pltpu

F32 = jnp.float32
BF16 = jnp.bfloat16

LANES = 128
SUBLANES = 8
VMEM_LIMIT = 56 * 1024 * 1024

RMS_EPS = 1e-6
ROPE_THETA = 500000.0
RWKV_HEAD_DIM = 64
RWKV_GN_EPS = 64e-5
WKV_CHUNK = 64
MLA_NOPE_DIM = 128
MLA_ROPE_DIM = 64
MLA_V_DIM = 128
MLA_QK_PAD = 256
MLA_IN_TILE = 256
DIFF_HEAD_DIM = 128
DIFF_ROT_DIM = DIFF_HEAD_DIM // 4
DIFF_EPS = 1e-5

NN = (((1,), (0,)), ((), ()))
NT = (((1,), (1,)), ((), ()))
TN = (((0,), (0,)), ((), ()))


def _blk(n, pref, align):
    if n <= pref:
        return n
    d = (pref // align) * align
    while d >= align:
        if n % d == 0:
            return d
        d -= align
    raise ValueError(f"no block for {n} (pref {pref}, align {align})")


def _params(*sem):
    return pltpu.CompilerParams(dimension_semantics=sem, vmem_limit_bytes=VMEM_LIMIT)


def _pad_to(x, axis, mult):
    n = x.shape[axis]
    pad = (-n) % mult
    if pad == 0:
        return x
    widths = [(0, 0)] * x.ndim
    widths[axis] = (0, pad)
    return jnp.pad(x, widths)


def _rmsnorm_body(x_ref, g_ref, o_ref, *, eps):
    x = x_ref[...].astype(F32)
    y = x * lax.rsqrt(jnp.mean(x * x, axis=-1, keepdims=True) + eps)
    o_ref[...] = (y * g_ref[...]).astype(o_ref.dtype)


def _rmsnorm(x, g, *, eps=RMS_EPS, out_dtype=F32, width=None, col_block=0):
    M = x.shape[0]
    width = x.shape[1] if width is None else width
    tm = _blk(M, 512, SUBLANES)
    return pl.pallas_call(
        functools.partial(_rmsnorm_body, eps=eps),
        out_shape=jax.ShapeDtypeStruct((M, width), out_dtype),
        grid=(M // tm,),
        in_specs=[pl.BlockSpec((tm, width), lambda i: (i, col_block)),
                  pl.BlockSpec((1, width), lambda i: (0, 0))],
        out_specs=pl.BlockSpec((tm, width), lambda i: (i, 0)),
        compiler_params=_params("parallel"),
        name="rmsnorm",
    )(x, g.reshape(1, width).astype(F32))


MIX_R, MIX_W, MIX_K, MIX_V, MIX_A, MIX_G = range(6)


def _premix_body(x_ref, prev_ref, next_ref, g_ref, mix_ref, *refs, eps, has_v1):
    n_w = 4 if has_v1 else 3
    wd1_ref, wa1_ref, wg1_ref = refs[0], refs[1], refs[2]
    wv1_ref = refs[3] if has_v1 else None
    xr_ref, xk_ref, xv_ref, tw_ref, ta_ref, tg_ref = refs[n_w:n_w + 6]
    tv_ref = refs[n_w + 6] if has_v1 else None
    i = pl.program_id(1)
    g = g_ref[...]

    def norm(v):
        return v * lax.rsqrt(jnp.mean(v * v, axis=-1, keepdims=True) + eps) * g

    h = norm(x_ref[0])
    tt = h.shape[0]
    h_before = jnp.where(i == 0, 0.0, norm(prev_ref[0])[SUBLANES - 1:SUBLANES])
    h_after = jnp.where(i == pl.num_programs(1) - 1, 0.0, norm(next_ref[0])[0:1])
    row = lax.broadcasted_iota(jnp.int32, (tt, 1), 0)
    h_dn = jnp.where(row == 0, h_before, pltpu.roll(h, 1, 0))
    h_up = jnp.where(row == tt - 1, h_after, pltpu.roll(h, tt - 1, 0))
    dx = 0.5 * (h_dn + h_up) - h

    def mixed(n):
        return (h + dx * mix_ref[n:n + 1, :]).astype(xr_ref.dtype)

    xr_ref[0] = mixed(MIX_R)
    xk_ref[0] = mixed(MIX_K)
    xv = mixed(MIX_V)
    xv_ref[0] = xv
    tw_ref[0] = jnp.tanh(jnp.dot(mixed(MIX_W), wd1_ref[...], preferred_element_type=F32)).astype(tw_ref.dtype)
    ta_ref[0] = jnp.dot(mixed(MIX_A), wa1_ref[...], preferred_element_type=F32).astype(ta_ref.dtype)
    tg_ref[0] = jax.nn.sigmoid(jnp.dot(mixed(MIX_G), wg1_ref[...], preferred_element_type=F32)).astype(tg_ref.dtype)
    if has_v1:
        tv_ref[0] = jnp.dot(xv, wv1_ref[...], preferred_element_type=F32).astype(tv_ref.dtype)


def _premix(x, g, mix, wd1, wa1, wg1, wv1=None, *, tt=256):
    B, T, C = x.shape
    tt = _blk(T, tt, SUBLANES)
    nb = tt // SUBLANES
    last = T // SUBLANES - 1
    weights = [wd1, wa1, wg1] + ([wv1] if wv1 is not None else [])
    tile = pl.BlockSpec((1, tt, C), lambda b, i: (b, i, 0))
    wide = jax.ShapeDtypeStruct((B, T, C), BF16)
    low_shapes = [jax.ShapeDtypeStruct((B, T, w.shape[1]), BF16) for w in weights]
    low_specs = [pl.BlockSpec((1, tt, w.shape[1]), lambda b, i: (b, i, 0)) for w in weights]
    return pl.pallas_call(
        functools.partial(_premix_body, eps=RMS_EPS, has_v1=wv1 is not None),
        out_shape=[wide] * 3 + low_shapes,
        grid=(B, T // tt),
        in_specs=[tile,
                  pl.BlockSpec((1, SUBLANES, C), lambda b, i: (b, jnp.maximum(i * nb - 1, 0), 0)),
                  pl.BlockSpec((1, SUBLANES, C), lambda b, i: (b, jnp.minimum((i + 1) * nb, last), 0)),
                  pl.BlockSpec((1, C), lambda b, i: (0, 0)),
                  pl.BlockSpec(mix.shape, lambda b, i: (0, 0))]
                 + [pl.BlockSpec(w.shape, lambda b, i: (0, 0)) for w in weights],
        out_specs=[tile] * 3 + low_specs,
        compiler_params=_params("parallel", "parallel"),
        name="rwkv_premix",
    )(x, x, x, g.reshape(1, C).astype(F32), mix.astype(F32), *weights)


def _rope_tables(T, dim):
    half = dim // 2
    inv_freq = 1.0 / (ROPE_THETA ** (jnp.arange(0, dim, 2, dtype=F32) / dim))
    ang = jnp.arange(T, dtype=F32)[:, None] * inv_freq[None, :]
    c, s = jnp.cos(ang), jnp.sin(ang)
    ones = jnp.ones((T, LANES - dim), F32)
    zeros = jnp.zeros((T, LANES - dim), F32)
    zh = jnp.zeros((T, half), F32)
    C = jnp.concatenate([c, c, ones], axis=1)
    S1 = jnp.concatenate([-s, zh, zeros], axis=1)
    S2 = jnp.concatenate([zh, s, zeros], axis=1)
    return C, S1, S2


def _rope_tile(t, c, s1, s2, half):
    return t * c + pltpu.roll(t, LANES - half, 1) * s1 + pltpu.roll(t, half, 1) * s2


def _rope_cols_body(x_ref, c_ref, s1_ref, s2_ref, o_ref, *, half):
    o_ref[...] = _rope_tile(x_ref[...].astype(F32), c_ref[...], s1_ref[...], s2_ref[...], half)


def _rope_cols(x, tables, T, col_block, half):
    M = x.shape[0]
    tm = _blk(T, 512, SUBLANES)
    nt = T // tm
    tab = pl.BlockSpec((tm, LANES), lambda i: (i % nt, 0))
    return pl.pallas_call(
        functools.partial(_rope_cols_body, half=half),
        out_shape=jax.ShapeDtypeStruct((M, LANES), F32),
        grid=(M // tm,),
        in_specs=[pl.BlockSpec((tm, LANES), lambda i: (i, col_block)), tab, tab, tab],
        out_specs=pl.BlockSpec((tm, LANES), lambda i: (i, 0)),
        compiler_params=_params("parallel"),
        name="rope_cols",
    )(x, *tables)


def _mm_body(*refs, n_extra, epilogue):
    x_ref, w_ref = refs[0], refs[1]
    extra = refs[2:2 + n_extra]
    o_ref = refs[2 + n_extra]
    acc = jnp.dot(x_ref[...], w_ref[...], preferred_element_type=F32)
    epilogue(acc, extra, o_ref)


def _epi_plain(acc, extra, o_ref):
    o_ref[...] = acc.astype(o_ref.dtype)


def _epi_residual(acc, extra, o_ref):
    o_ref[...] = (extra[0][...] + acc).astype(o_ref.dtype)


def _epi_value_residual(acc, extra, o_ref):
    v = extra[0][...]
    o_ref[...] = (v + (extra[1][...] - v) * jax.nn.sigmoid(extra[2][...] + acc)).astype(o_ref.dtype)


def _epi_rope_second_tile(acc, extra, o_ref, *, half, qscale):
    c, s1, s2 = (e[...] for e in extra)
    acc = acc * qscale
    for t in range(acc.shape[1] // LANES):
        sl = slice(t * LANES, (t + 1) * LANES)
        tile = _rope_tile(acc[:, sl], c, s1, s2, half) if t % 2 else acc[:, sl]
        o_ref[:, sl] = tile.astype(o_ref.dtype)


def _epi_rope_tiles(acc, extra, o_ref, *, half, n_rope_blocks, n_q_blocks, qscale):
    j = pl.program_id(1)

    @pl.when(j < n_rope_blocks)
    def _():
        c, s1, s2 = (e[...] for e in extra)
        scaled = acc * jnp.where(j < n_q_blocks, qscale, 1.0).astype(F32)
        for t in range(acc.shape[1] // LANES):
            sl = slice(t * LANES, (t + 1) * LANES)
            o_ref[:, sl] = _rope_tile(scaled[:, sl], c, s1, s2, half).astype(o_ref.dtype)

    @pl.when(j >= n_rope_blocks)
    def _():
        o_ref[...] = acc.astype(o_ref.dtype)


def _epi_add_second_tile(acc, extra, o_ref, *, n_add_blocks):
    j = pl.program_id(1)

    @pl.when(j < n_add_blocks)
    def _():
        shared = extra[0][...]
        for t in range(acc.shape[1] // LANES):
            sl = slice(t * LANES, (t + 1) * LANES)
            tile = acc[:, sl] + shared if t % 2 else acc[:, sl]
            o_ref[:, sl] = tile.astype(o_ref.dtype)

    @pl.when(j >= n_add_blocks)
    def _():
        o_ref[...] = acc.astype(o_ref.dtype)


X_TILE_BYTES = 8 * 1024 * 1024


def _mm_tiles(M, K, N, tm, tn):
    if tm is None:
        tm = 1024 if 1024 * K * 2 <= X_TILE_BYTES else 512
    if tn is None:
        tn = 512
    return _blk(M, tm, SUBLANES), _blk(N, tn, LANES)


def _matmul(x, w, *, out_dtype, epilogue=None, extra=(), extra_specs=(), tm=None, tn=None,
            name="matmul"):
    M = x.shape[0]
    K, N = w.shape
    tm, tn = _mm_tiles(M, K, N, tm, tn)
    if epilogue is None:
        epilogue = _epi_plain
    return pl.pallas_call(
        functools.partial(_mm_body, n_extra=len(extra), epilogue=epilogue),
        out_shape=jax.ShapeDtypeStruct((M, N), out_dtype),
        grid=(M // tm, N // tn),
        in_specs=[pl.BlockSpec((tm, K), lambda i, j: (i, 0)),
                  pl.BlockSpec((K, tn), lambda i, j: (0, j))] + list(extra_specs),
        out_specs=pl.BlockSpec((tm, tn), lambda i, j: (i, j)),
        compiler_params=_params("parallel", "arbitrary"),
        name=name,
    )(x, w, *extra)


def _matmul_residual(x, w, res, *, tm=None, tn=None, name="matmul_residual"):
    M = x.shape[0]
    K, N = w.shape
    tm_, tn_ = _mm_tiles(M, K, N, tm, tn)
    return _matmul(x, w, out_dtype=F32, epilogue=_epi_residual, extra=(res,),
                   extra_specs=(pl.BlockSpec((tm_, tn_), lambda i, j: (i, j)),), tm=tm, tn=tn, name=name)


def _table_specs(T, tm):
    nt = T // tm
    return tuple(pl.BlockSpec((tm, LANES), lambda i, j: (i % nt, 0)) for _ in range(3))


def _swiglu_body(x_ref, wg_ref, wu_ref, o_ref):
    x = x_ref[...]
    g = jnp.dot(x, wg_ref[...], preferred_element_type=F32)
    u = jnp.dot(x, wu_ref[...], preferred_element_type=F32)
    o_ref[...] = (g * jax.nn.sigmoid(g) * u).astype(o_ref.dtype)


def _swiglu_up(x, w_gu, hidden, *, tm=2048, tn=256):
    M, K = x.shape
    tm = _blk(M, tm, SUBLANES)
    tn = _blk(hidden, tn, LANES)
    nb = hidden // tn
    return pl.pallas_call(
        _swiglu_body,
        out_shape=jax.ShapeDtypeStruct((M, hidden), BF16),
        grid=(M // tm, nb),
        in_specs=[pl.BlockSpec((tm, K), lambda i, j: (i, 0), pipeline_mode=pl.Buffered(1)),
                  pl.BlockSpec((K, tn), lambda i, j: (0, j)),
                  pl.BlockSpec((K, tn), lambda i, j: (0, j + nb))],
        out_specs=pl.BlockSpec((tm, tn), lambda i, j: (i, j)),
        compiler_params=_params("parallel", "arbitrary"),
        name="swiglu_up",
    )(x, w_gu, w_gu)


def _flash_body(*refs, n_maps, dqk, dv, diff, lambda_init, bks):
    if diff:
        q_ref, k_ref, vt_ref, lam_ref, g_ref, o_ref, m_scr, acc_scr = refs
    else:
        q_ref, k_ref, vt_ref, o_ref, m_scr, acc_scr = refs
    j = pl.program_id(3)

    @pl.when(j == 0)
    def _():
        m_scr[...] = jnp.full(m_scr.shape, -jnp.inf, F32)
        acc_scr[...] = jnp.zeros(acc_scr.shape, F32)

    q = q_ref[0]
    maps = [slice(mp * dqk, (mp + 1) * dqk) for mp in range(n_maps)]
    m = [m_scr[mp] for mp in range(n_maps)]
    acc = [acc_scr[mp] for mp in range(n_maps)]
    n_sub = k_ref.shape[1] // bks
    st = []
    for sub in range(n_sub):
        k = k_ref[0, sub * bks:(sub + 1) * bks, :]
        st.append([lax.dot_general(k[:, sl], q[:, sl], NT, preferred_element_type=F32) for sl in maps])
    for sub in range(n_sub):
        vt = vt_ref[0, :, sub * bks:(sub + 1) * bks]
        for mp in range(n_maps):
            s_t = st[sub][mp]
            m_new = jnp.maximum(m[mp], jnp.max(s_t, axis=0, keepdims=True))
            alpha = jnp.exp2(m[mp] - m_new)
            p = jnp.exp2(s_t - m_new).astype(vt.dtype)
            acc[mp] = alpha * acc[mp] + jnp.dot(vt, p, preferred_element_type=F32)
            m[mp] = m_new
    for mp in range(n_maps):
        m_scr[mp] = m[mp]
        acc_scr[mp] = acc[mp]

    @pl.when(j == pl.num_programs(3) - 1)
    def _():
        o = [(a[:dv] / a[dv:dv + 1]).T for a in acc]
        if diff:
            lp = lam_ref[...]
            lam = (jnp.exp(jnp.sum(lp[0:1] * lp[1:2], axis=-1, keepdims=True))
                   - jnp.exp(jnp.sum(lp[2:3] * lp[3:4], axis=-1, keepdims=True)) + lambda_init)
            od = o[0] - lam * o[1]
            y = od * lax.rsqrt(jnp.mean(od * od, axis=-1, keepdims=True) + DIFF_EPS) * g_ref[...]
            o_ref[0] = (y * (1.0 - lambda_init)).astype(o_ref.dtype)
        else:
            o_ref[0] = o[0].astype(o_ref.dtype)


ONES_ROWS = 16


def _v_transposed(v, heads, dv):
    B, T = v.shape[0], v.shape[1]
    vt = jnp.swapaxes(v, 1, 2).reshape(B, heads, dv, T)
    ones = jnp.ones((B, heads, ONES_ROWS, T), v.dtype)
    return jnp.concatenate([vt, ones], axis=2).reshape(B, heads * (dv + ONES_ROWS), T)


def _flash(q_arr, k_arr, v_arr, *, heads, n_maps, dqk, dv, q_blk0, k_blk0,
           diff_params=None, lambda_init=0.0, bq=512, bk=8192, bks=256):
    B, T = q_arr.shape[0], q_arr.shape[1]
    bq = _blk(T, bq, LANES)
    bk = _blk(T, bk, LANES)
    bks = _blk(bk, bks, LANES)
    wqk = n_maps * dqk
    dve = dv + ONES_ROWS
    diff = diff_params is not None
    in_specs = [pl.BlockSpec((1, bq, wqk), lambda b, h, i, j: (b, i, q_blk0 + h)),
                pl.BlockSpec((1, bk, wqk), lambda b, h, i, j: (b, j, k_blk0 + h)),
                pl.BlockSpec((1, dve, bk), lambda b, h, i, j: (b, h, j))]
    args = [q_arr, k_arr, _v_transposed(v_arr, heads, dv)]
    if diff:
        lam, subln = diff_params
        in_specs += [pl.BlockSpec(lam.shape, lambda b, h, i, j: (0, 0)),
                     pl.BlockSpec(subln.shape, lambda b, h, i, j: (0, 0))]
        args += [lam, subln]
    return pl.pallas_call(
        functools.partial(_flash_body, n_maps=n_maps, dqk=dqk, dv=dv, diff=diff,
                          lambda_init=lambda_init, bks=bks),
        out_shape=jax.ShapeDtypeStruct((B, T, heads * dv), BF16),
        grid=(B, heads, T // bq, T // bk),
        in_specs=in_specs,
        out_specs=pl.BlockSpec((1, bq, dv), lambda b, h, i, j: (b, i, h)),
        scratch_shapes=[pltpu.VMEM((n_maps, 1, bq), F32),
                        pltpu.VMEM((n_maps, dve, bq), F32)],
        compiler_params=_params("parallel", "parallel", "parallel", "arbitrary"),
        name="flash_diff" if diff else "flash_mla",
    )(*args)


def _split_dot(m_exact, x, n_parts):
    acc = None
    rem = x
    for _ in range(n_parts):
        part = rem.astype(BF16)
        rem = rem - part.astype(F32)
        d = lax.dot_general(m_exact, part, NN, preferred_element_type=F32)
        acc = d if acc is None else acc + d
    return acc


def _wkv_body(r_ref, k_ref, v_ref, tw_ref, wd2_ref, ta_ref, wa2_ref, w0_ref, a0_ref, kk_ref, ka_ref,
              o_ref, s_scr,
              *, reverse, L, pairs):
    c = pl.program_id(2)

    @pl.when(c == 0)
    def _():
        s_scr[...] = jnp.zeros(s_scr.shape, F32)

    L2 = 2 * L
    ri = lax.broadcasted_iota(jnp.int32, (L, L), 0)
    ci = lax.broadcasted_iota(jnp.int32, (L, L), 1)
    tri = ((ri <= ci) if reverse else (ri >= ci)).astype(BF16)
    r2 = lax.broadcasted_iota(jnp.int32, (L2, L2), 0)
    c2 = lax.broadcasted_iota(jnp.int32, (L2, L2), 1)
    strict = (r2 < c2) if reverse else (r2 > c2)
    incl = (r2 <= c2) if reverse else (r2 >= c2)
    eye = jnp.where(r2 == c2, 1.0, 0.0).astype(F32)
    lane = lax.broadcasted_iota(jnp.int32, (L, LANES), 1)
    head0 = lane < RWKV_HEAD_DIM
    hr = lax.broadcasted_iota(jnp.int32, (LANES, LANES), 0) // RWKV_HEAD_DIM
    hc = lax.broadcasted_iota(jnp.int32, (LANES, LANES), 1) // RWKV_HEAD_DIM
    head_ones = jnp.where(hr == hc, 1.0, 0.0).astype(BF16)

    def stack(q):
        return jnp.concatenate([jnp.where(head0, q, 0.0), jnp.where(head0, 0.0, q)], axis=0).astype(BF16)

    dotf = lambda a, b, dims: lax.dot_general(a, b, dims, preferred_element_type=F32)
    P = range(pairs)
    lanes = [slice(p * LANES, (p + 1) * LANES) for p in P]

    zw = jnp.dot(tw_ref[0], wd2_ref[...], preferred_element_type=F32)
    za = jnp.dot(ta_ref[0], wa2_ref[...], preferred_element_type=F32)
    lw = -math.exp(-0.5) * jax.nn.sigmoid(w0_ref[...] + zw)
    asig = jax.nn.sigmoid(a0_ref[...] + za)
    k_all = k_ref[0]
    kkr = k_all * kk_ref[...]
    kkr2 = kkr * kkr
    n2 = jnp.concatenate([_head_sum(kkr2[:, sl], head_ones) for sl in lanes], axis=1)
    kk = kkr / jnp.maximum(jnp.sqrt(n2), 1e-12)
    kd = k_all * (1.0 + (asig - 1.0) * ka_ref[...])
    b_all = kk * asig
    cum = _split_dot(tri, lw, 3)
    tot = jnp.sum(lw, axis=0, keepdims=True)
    a_t = -kk * jnp.exp(cum - lw)
    r_t = r_ref[0] * jnp.exp(cum)
    pinv = jnp.exp(-cum)
    pend = jnp.exp(tot - cum)
    decay = jnp.exp(tot)
    b_in, k_in = b_all * pinv, kd * pinv
    b_end, k_end = b_all * pend, kd * pend

    As = [stack(a_t[:, sl]) for sl in lanes]
    Bs = [stack(b_in[:, sl]) for sl in lanes]
    m_ab = [jnp.where(strict, dotf(As[p], Bs[p], NT), 0.0) for p in P]

    inv = None
    s = 1
    while s < L:
        sh = s.bit_length() - 1
        off_diag = ((r2 >> (sh + 1)) == (c2 >> (sh + 1))) & ((r2 >> sh) != (c2 >> sh))
        e = [jnp.where(off_diag, m_ab[p], 0.0) for p in P]
        if s == 1:
            inv = [eye + e[p] for p in P]
        else:
            invb = [inv[p].astype(BF16) for p in P]
            t = [dotf(e[p].astype(BF16), invb[p], NN).astype(BF16) for p in P]
            inv = [inv[p] + dotf(invb[p], t[p], NN) for p in P]
        s *= 2

    Rs = [stack(r_t[:, sl]) for sl in lanes]
    Ks = [stack(k_in[:, sl]) for sl in lanes]
    Vs = [stack(v_ref[0, :, sl]) for sl in lanes]
    m_ak = [jnp.where(strict, dotf(As[p], Ks[p], NT), 0.0).astype(BF16) for p in P]
    m_r = [jnp.concatenate([jnp.where(incl, dotf(Rs[p], Bs[p], NT), 0.0),
                            jnp.where(incl, dotf(Rs[p], Ks[p], NT), 0.0)], axis=1).astype(BF16) for p in P]
    state = [s_scr[p] for p in P]
    sb = [state[p].astype(BF16) for p in P]
    w_s = [dotf(As[p], sb[p], NT) + dotf(m_ak[p], Vs[p], NN) for p in P]
    u_s = [dotf(inv[p].astype(BF16), w_s[p].astype(BF16), NN).astype(BF16) for p in P]
    uv = [jnp.concatenate([u_s[p], Vs[p]], axis=0) for p in P]
    o_s = [dotf(Rs[p], sb[p], NT) + dotf(m_r[p], uv[p], NN) for p in P]
    bk_end = [jnp.concatenate([stack(b_end[:, sl]), stack(k_end[:, sl])], axis=0) for sl in lanes]
    new_state = [state[p] * decay[:, lanes[p]] + dotf(uv[p], bk_end[p], TN) for p in P]
    o_ref[0] = jnp.concatenate([o[:L] + o[L:] for o in o_s], axis=1)
    s_scr[...] = jnp.stack(new_state)


def _head_sum(x, head_ones):
    hi = x.astype(BF16)
    lo = (x - hi.astype(F32)).astype(BF16)
    return (lax.dot_general(hi, head_ones, NN, preferred_element_type=F32)
            + lax.dot_general(lo, head_ones, NN, preferred_element_type=F32))


def _wkv(r, k, v, tw, wd2, ta, wa2, w0, a0, k_k, k_a, *, direction, pairs=32):
    B, T, C = r.shape
    L = WKV_CHUNK
    pairs = min(pairs, C // LANES)
    wl = pairs * LANES
    nc = T // L
    R = wd2.shape[0]
    reverse = direction == 1

    def chunk(c):
        return (nc - 1 - c) if reverse else c

    seq = pl.BlockSpec((1, L, wl), lambda b, g, c: (b, chunk(c), g))
    low = pl.BlockSpec((1, L, R), lambda b, g, c: (b, chunk(c), direction))
    fac = pl.BlockSpec((R, wl), lambda b, g, c: (0, g))
    par = pl.BlockSpec((1, wl), lambda b, g, c: (0, g))
    return pl.pallas_call(
        functools.partial(_wkv_body, reverse=reverse, L=L, pairs=pairs),
        out_shape=jax.ShapeDtypeStruct((B, T, C), F32),
        grid=(B, C // wl, nc),
        in_specs=[seq, seq, seq, low, fac, low, fac, par, par, par, par],
        out_specs=seq,
        scratch_shapes=[pltpu.VMEM((pairs, LANES, LANES), F32)],
        compiler_params=_params("parallel", "parallel", "arbitrary"),
        name="wkv7_rev" if reverse else "wkv7_fwd",
    )(r, k, v, tw, wd2, ta, wa2, w0, a0, k_k, k_a)


def _wkv_out_body(of_ref, ob_ref, r_ref, k_ref, v_ref, ta_ref, wa20_ref, wa21_ref, tg_ref, wg2_ref,
                  a00_ref, a01_ref, ka_ref, rk_ref, lw_ref, lb_ref, o_ref):
    hr = lax.broadcasted_iota(jnp.int32, (LANES, LANES), 0) // RWKV_HEAD_DIM
    hc = lax.broadcasted_iota(jnp.int32, (LANES, LANES), 1) // RWKV_HEAD_DIM
    head_ones = jnp.where(hr == hc, 1.0, 0.0).astype(BF16)
    inv_n = 1.0 / RWKV_HEAD_DIM
    R = wa20_ref.shape[0]
    za0 = jnp.dot(ta_ref[:, :R], wa20_ref[...], preferred_element_type=F32)
    za1 = jnp.dot(ta_ref[:, R:], wa21_ref[...], preferred_element_type=F32)
    asum_all = jax.nn.sigmoid(a00_ref[...] + za0) + jax.nn.sigmoid(a01_ref[...] + za1)
    gate = jnp.dot(tg_ref[...], wg2_ref[...], preferred_element_type=F32)
    for t in range(o_ref.shape[1] // LANES):
        sl = slice(t * LANES, (t + 1) * LANES)
        o = of_ref[:, sl] + ob_ref[:, sl]
        mu = _head_sum(o, head_ones) * inv_n
        d = o - mu
        var = _head_sum(d * d, head_ones) * inv_n
        on = d * lax.rsqrt(var + RWKV_GN_EPS) * lw_ref[:, sl] + lb_ref[:, sl]
        asum = asum_all[:, sl]
        ksum = k_ref[:, sl] * (2.0 + (asum - 2.0) * ka_ref[:, sl])
        bonus = _head_sum(r_ref[:, sl] * ksum * rk_ref[:, sl], head_ones) * v_ref[:, sl]
        o_ref[:, sl] = ((on + bonus) * gate[:, sl]).astype(o_ref.dtype)


def _wkv_out(o_f, o_b, r, k, v, ta, wa2, tg, wg2, a0, k_a, r_k, lnx_w, lnx_b):
    M, C = r.shape
    tm = _blk(M, 512, SUBLANES)
    tc = _blk(C, 512, LANES)
    R = wa2[0].shape[0]
    Rg = wg2.shape[0]
    seq = pl.BlockSpec((tm, tc), lambda i, j: (i, j))
    par = pl.BlockSpec((1, tc), lambda i, j: (0, j))
    row = lambda x: x.reshape(1, C).astype(F32)
    return pl.pallas_call(
        _wkv_out_body,
        out_shape=jax.ShapeDtypeStruct((M, C), BF16),
        grid=(M // tm, C // tc),
        in_specs=[seq] * 5
                 + [pl.BlockSpec((tm, 2 * R), lambda i, j: (i, 0)),
                    pl.BlockSpec((R, tc), lambda i, j: (0, j)),
                    pl.BlockSpec((R, tc), lambda i, j: (0, j)),
                    pl.BlockSpec((tm, Rg), lambda i, j: (i, 0)),
                    pl.BlockSpec((Rg, tc), lambda i, j: (0, j))]
                 + [par] * 6,
        out_specs=seq,
        compiler_params=_params("parallel", "parallel"),
        name="wkv_out",
    )(o_f, o_b, r, k, v, ta, wa2[0], wa2[1], tg, wg2,
      row(a0[0]), row(a0[1]), row(k_a), row(r_k), row(lnx_w), row(lnx_b))


def _ffn(x2, p):
    h = _rmsnorm(x2, p["norm_ffn"], out_dtype=BF16)
    hidden = p["w_down"].shape[0]
    mid = _swiglu_up(h, p["w_gu"], hidden)
    return _matmul_residual(mid, p["w_down"], x2, tm=512, tn=256, name="ffn_down")


def _rwkv_layer(x, p, v_first):
    B, T, C = x.shape
    M = B * T
    x2 = x.reshape(M, C)
    flat = lambda t: t.reshape(M, t.shape[2])
    xr, xk, xv, tw, ta, tg, *tv = map(flat, _premix(x, p["norm_mix"], p["mix"], p["decay1"], p["a1"], p["g1"],
                                                      p.get("v1")))
    r = _matmul(xr, p["w_r"], out_dtype=F32, name="rwkv_r")
    k = _matmul(xk, p["w_k"], out_dtype=F32, name="rwkv_k")
    v = _matmul(xv, p["w_v"], out_dtype=F32, name="rwkv_v")
    if v_first is not None:
        tv = tv[0]
        tm_v, tn_v = _mm_tiles(M, p["v2"].shape[0], C, None, None)
        tile = pl.BlockSpec((tm_v, tn_v), lambda i, j: (i, j))
        v = _matmul(tv, p["v2"], out_dtype=F32, epilogue=_epi_value_residual,
                    extra=(v, v_first, p["v0"].reshape(1, C).astype(F32)),
                    extra_specs=(tile, tile, pl.BlockSpec((1, tn_v), lambda i, j: (0, j))), name="rwkv_v2")
    seq = lambda t: t.reshape(B, T, C)
    row = lambda t: t.reshape(1, C).astype(F32)
    low = lambda t: t.reshape(B, T, t.shape[1])
    o_dir = [
        _wkv(seq(r), seq(k), seq(v), low(tw), p["decay2"][d], low(ta), p["a2"][d],
             row(p["w_decay0"][d]), row(p["a0"][d]), row(p["k_k"]), row(p["k_a"]), direction=d)
        for d in range(2)
    ]
    y = _wkv_out(o_dir[0].reshape(M, C), o_dir[1].reshape(M, C), r, k, v, ta, p["a2"], tg, p["g2"],
                 p["a0"], p["k_a"], p["r_k"], p["lnx_w"], p["lnx_b"])
    x2 = _matmul_residual(y, p["w_o"], x2, name="rwkv_o")
    return x2, v


def _mla_layer(x, p):
    B, T, C = x.shape
    M = B * T
    x2 = x.reshape(M, C)
    H = p["w_uq"].shape[1] // MLA_QK_PAD
    q_rank = p["q_norm"].shape[0]
    kv_rank = p["kv_norm"].shape[0]
    h = _rmsnorm(x2, p["norm_mix"], out_dtype=BF16)
    lat = _matmul(h, p["w_in"], out_dtype=F32, tn=MLA_IN_TILE, name="mla_in")
    c_q = _rmsnorm(lat, p["q_norm"], out_dtype=BF16, width=q_rank, col_block=0)
    c_kv = _rmsnorm(lat, p["kv_norm"], out_dtype=BF16, width=kv_rank, col_block=q_rank // kv_rank)
    tables = _rope_tables(T, MLA_ROPE_DIM)
    half = MLA_ROPE_DIM // 2
    k_rope = _rope_cols(lat, tables, T, (q_rank + kv_rank) // LANES, half)
    tm = _blk(T, 1024, SUBLANES)
    qscale = (MLA_NOPE_DIM + MLA_ROPE_DIM) ** -0.5 * math.log2(math.e)
    q = _matmul(c_q, p["w_uq"], out_dtype=BF16, tm=tm, tn=_blk(H * MLA_QK_PAD, 1024, MLA_QK_PAD),
                epilogue=functools.partial(_epi_rope_second_tile, half=half, qscale=qscale),
                extra=tables, extra_specs=_table_specs(T, _blk(M, tm, SUBLANES)), name="mla_uq")
    tm_kv = _blk(M, 1024, SUBLANES)
    tn_kv = _blk(H * MLA_V_DIM, 2048, MLA_QK_PAD)
    kv = _matmul(c_kv, p["w_ukv"], out_dtype=BF16, tm=tm_kv, tn=tn_kv,
                 epilogue=functools.partial(_epi_add_second_tile, n_add_blocks=H * MLA_QK_PAD // tn_kv),
                 extra=(k_rope,), extra_specs=(pl.BlockSpec((tm_kv, LANES), lambda i, j: (i, 0)),),
                 name="mla_ukv")
    q3 = q.reshape(B, T, H * MLA_QK_PAD)
    kv3 = kv.reshape(B, T, kv.shape[1])
    o = _flash(q3, kv3, kv3[:, :, H * MLA_QK_PAD:], heads=H, n_maps=1, dqk=MLA_QK_PAD, dv=MLA_V_DIM,
               q_blk0=0, k_blk0=0)
    return _matmul_residual(o.reshape(M, H * MLA_V_DIM), p["w_o"], x2, name="mla_o")


def _diff_layer(x, p, lambda_init):
    B, T, C = x.shape
    M = B * T
    x2 = x.reshape(M, C)
    H = C // (2 * DIFF_HEAD_DIM)
    h = _rmsnorm(x2, p["norm_mix"], out_dtype=BF16)
    tables = _rope_tables(T, DIFF_ROT_DIM)
    tm = _blk(T, 1024, SUBLANES)
    tn = _blk(C, 512, LANES)
    qkv = _matmul(h, p["w_qkv"], out_dtype=BF16, tm=tm, tn=tn,
                  epilogue=functools.partial(_epi_rope_tiles, half=DIFF_ROT_DIM // 2,
                                             n_rope_blocks=2 * C // tn, n_q_blocks=C // tn,
                                             qscale=DIFF_HEAD_DIM ** -0.5 * math.log2(math.e)),
                  extra=tables, extra_specs=_table_specs(T, _blk(M, tm, SUBLANES)), name="diff_qkv")
    qkv3 = qkv.reshape(B, T, 3 * C)
    o = _flash(qkv3, qkv3, qkv3[:, :, 2 * C:], heads=H, n_maps=2, dqk=DIFF_HEAD_DIM,
               dv=2 * DIFF_HEAD_DIM, q_blk0=0, k_blk0=H,
               diff_params=(p["lambda"].astype(F32), p["subln"].reshape(1, -1).astype(F32)),
               lambda_init=lambda_init)
    return _matmul_residual(o.reshape(M, C), p["w_o"], x2, name="diff_o")


def _trunk(x, layers, final_norm):
    B, T, C = x.shape
    v_first = None
    for i, p in enumerate(layers):
        kind = p["kind"]
        if kind == "rwkv":
            x2, v_layer = _rwkv_layer(x, p, v_first)
            if v_first is None:
                v_first = v_layer
        elif kind == "mla":
            x2 = _mla_layer(x, p)
        else:
            x2 = _diff_layer(x, p, 0.8 - 0.6 * math.exp(-0.3 * i))
        x2 = _ffn(x2, p)
        x = x2.reshape(B, T, C)
    return _rmsnorm(x.reshape(B * T, C), final_norm).reshape(B, T, C)


def _prep_ffn(p):
    return {"norm_ffn": p["norm_ffn"], "w_gu": p["w_gu"].astype(BF16), "w_down": p["w_down"].astype(BF16)}


def _prep_rwkv(p):
    b = lambda w: w.astype(BF16)
    out = dict(kind="rwkv", norm_mix=p["norm_mix"], mix=p["mix"],
               w_r=b(p["w_rkv"][0]), w_k=b(p["w_rkv"][1]), w_v=b(p["w_rkv"][2]),
               w_decay0=p["w_decay0"], a0=p["a0"], k_k=p["k_k"], k_a=p["k_a"], r_k=p["r_k"],
               lnx_w=p["lnx_w"], lnx_b=p["lnx_b"], w_o=b(p["w_o"]))
    for name1, name2, key1, key2 in (("w_decay1", "w_decay2", "decay1", "decay2"), ("a1", "a2", "a1", "a2")):
        w1 = [_pad_to(p[name1][d], 1, LANES) for d in range(2)]
        out[key1] = b(jnp.concatenate(w1, axis=1))
        out[key2] = [b(_pad_to(p[name2][d], 0, LANES)) for d in range(2)]
    if "v0" in p:
        out["v0"] = p["v0"]
        out["v1"] = b(_pad_to(p["v1"], 1, LANES))
        out["v2"] = b(_pad_to(p["v2"], 0, LANES))
    out["g1"] = b(_pad_to(p["g1"], 1, LANES))
    out["g2"] = b(_pad_to(p["g2"], 0, LANES))
    out.update(_prep_ffn(p))
    return out


def _prep_mla(p):
    b = lambda w: w.astype(BF16)
    q_rank = p["q_norm"].shape[0]
    kv_rank = p["kv_norm"].shape[0]
    qk = MLA_NOPE_DIM + MLA_ROPE_DIM
    H = p["w_uq"].shape[1] // qk
    w_uq = _pad_to(p["w_uq"].reshape(q_rank, H, qk), 2, MLA_QK_PAD).reshape(q_rank, H * MLA_QK_PAD)
    w_ukv = p["w_ukv"].reshape(kv_rank, H, MLA_NOPE_DIM + MLA_V_DIM)
    w_k = _pad_to(w_ukv[:, :, :MLA_NOPE_DIM], 2, MLA_QK_PAD).reshape(kv_rank, H * MLA_QK_PAD)
    w_v = w_ukv[:, :, MLA_NOPE_DIM:].reshape(kv_rank, H * MLA_V_DIM)
    out = dict(kind="mla", norm_mix=p["norm_mix"], w_in=b(_pad_to(p["w_in"], 1, MLA_IN_TILE)),
               q_norm=p["q_norm"], kv_norm=p["kv_norm"], w_uq=b(w_uq),
               w_ukv=b(jnp.concatenate([w_k, w_v], axis=1)), w_o=b(p["w_o"]))
    out.update(_prep_ffn(p))
    return out


def _prep_diff(p):
    out = dict(kind="diff", norm_mix=p["norm_mix"], w_qkv=p["w_qkv"].astype(BF16), subln=p["subln"],
               w_o=p["w_o"].astype(BF16))
    out["lambda"] = p["lambda"]
    out.update(_prep_ffn(p))
    return out


def _forward(x_prompt, x_sample, raw_layers, final_norm):
    prep = {"rwkv": _prep_rwkv, "mla": _prep_mla, "diff": _prep_diff}
    kinds = ("rwkv", "mla", "diff")
    layers = [prep[kinds[i % 3]](p) for i, p in enumerate(raw_layers)]
    return (_trunk(x_prompt, layers, final_norm), _trunk(x_sample, layers, final_norm))


def kernel(x_prompt, x_sample, l0_norm_mix, l0_mix, l0_w_rkv, l0_w_decay0, l0_w_decay1, l0_w_decay2, l0_a0, l0_a1, l0_a2, l0_g1, l0_g2, l0_k_k, l0_k_a, l0_r_k, l0_lnx_w, l0_lnx_b, l0_w_o, l0_norm_ffn, l0_w_gu, l0_w_down, l1_norm_mix, l1_w_in, l1_q_norm, l1_kv_norm, l1_w_uq, l1_w_ukv, l1_w_o, l1_norm_ffn, l1_w_gu, l1_w_down, l2_norm_mix, l2_w_qkv, l2_lambda, l2_subln, l2_w_o, l2_norm_ffn, l2_w_gu, l2_w_down, l3_norm_mix, l3_mix, l3_w_rkv, l3_w_decay0, l3_w_decay1, l3_w_decay2, l3_a0, l3_a1, l3_a2, l3_v0, l3_v1, l3_v2, l3_g1, l3_g2, l3_k_k, l3_k_a, l3_r_k, l3_lnx_w, l3_lnx_b, l3_w_o, l3_norm_ffn, l3_w_gu, l3_w_down, final_norm):
    raw_layers = (
        dict(norm_mix=l0_norm_mix, mix=l0_mix, w_rkv=l0_w_rkv, w_decay0=l0_w_decay0, w_decay1=l0_w_decay1,
             w_decay2=l0_w_decay2, a0=l0_a0, a1=l0_a1, a2=l0_a2, g1=l0_g1, g2=l0_g2, k_k=l0_k_k, k_a=l0_k_a,
             r_k=l0_r_k, lnx_w=l0_lnx_w, lnx_b=l0_lnx_b, w_o=l0_w_o,
             norm_ffn=l0_norm_ffn, w_gu=l0_w_gu, w_down=l0_w_down),
        dict(norm_mix=l1_norm_mix, w_in=l1_w_in, q_norm=l1_q_norm, kv_norm=l1_kv_norm, w_uq=l1_w_uq,
             w_ukv=l1_w_ukv, w_o=l1_w_o, norm_ffn=l1_norm_ffn, w_gu=l1_w_gu, w_down=l1_w_down),
        dict(norm_mix=l2_norm_mix, w_qkv=l2_w_qkv, subln=l2_subln, w_o=l2_w_o,
             norm_ffn=l2_norm_ffn, w_gu=l2_w_gu, w_down=l2_w_down, **{"lambda": l2_lambda}),
        dict(norm_mix=l3_norm_mix, mix=l3_mix, w_rkv=l3_w_rkv, w_decay0=l3_w_decay0, w_decay1=l3_w_decay1,
             w_decay2=l3_w_decay2, a0=l3_a0, a1=l3_a1, a2=l3_a2, v0=l3_v0, v1=l3_v1, v2=l3_v2,
             g1=l3_g1, g2=l3_g2, k_k=l3_k_k, k_a=l3_k_a, r_k=l3_r_k, lnx_w=l3_lnx_w, lnx_b=l3_lnx_b, w_o=l3_w_o,
             norm_ffn=l3_norm_ffn, w_gu=l3_w_gu, w_down=l3_w_down),
    )
    return _forward(x_prompt, x_sample, raw_layers, final_norm)
```
